```python
import math
import jax, jax.numpy as jnp
from jax import lax
import numpy as np

D_MODEL = 2048
BATCH = 4
SEQ = 4096
DEPTH = 4

N_MIXERS = 3
N_SSD_LAYERS = (DEPTH + 2) // 3
N_MLA_LAYERS = (DEPTH + 1) // 3
N_GDN_LAYERS = DEPTH // 3
DEEPNORM_ALPHA = (2.0 * DEPTH) ** 0.25
DEEPNORM_BETA = (8.0 * DEPTH) ** -0.25
LN_EPS = 1e-5
RMS_EPS = 1e-6

SSD_D_INNER = 2 * D_MODEL
SSD_HEAD_DIM = 64
SSD_N_HEADS = SSD_D_INNER // SSD_HEAD_DIM
SSD_N_GROUPS = 8
SSD_D_STATE = 128
SSD_CONV = 4
SSD_CHUNK = 128
SSD_CONV_DIM = SSD_D_INNER + 2 * SSD_N_GROUPS * SSD_D_STATE
SSD_PROJ = SSD_D_INNER + SSD_CONV_DIM + SSD_N_HEADS

MLA_N_HEADS = D_MODEL // 128
MLA_Q_RANK = 768
MLA_KV_RANK = 512
MLA_NOPE = 128
MLA_ROPE = 64
MLA_V = 128
MLA_GATE = MLA_N_HEADS * MLA_V
MLA_PROJ = MLA_Q_RANK + MLA_KV_RANK + MLA_ROPE + MLA_GATE
MLA_Q_BLOCK = 128
ROPE_THETA = 10000.0

GDN_N_QK_HEADS = 16
GDN_N_V_HEADS = 32
GDN_DK = 128
GDN_DV = 128
GDN_KEY_DIM = GDN_N_QK_HEADS * GDN_DK
GDN_VAL_DIM = GDN_N_V_HEADS * GDN_DV
GDN_CONV = 4
GDN_CHUNK = 64
GDN_CONV_DIM = 2 * GDN_KEY_DIM + GDN_VAL_DIM
GDN_PROJ = GDN_CONV_DIM + GDN_VAL_DIM + 2 * GDN_N_V_HEADS

kernel_name = "hybrid_ssd_mla_gdn_deepnorm"


def _layernorm(x, g, b):
    xf = x.astype(jnp.float32)
    mu = jnp.mean(xf, -1, keepdims=True)
    var = jnp.mean(jnp.square(xf - mu), -1, keepdims=True)
    return ((xf - mu) * lax.rsqrt(var + LN_EPS) * g + b).astype(x.dtype)


def _rmsnorm(x, w):
    xf = x.astype(jnp.float32)
    return (xf * lax.rsqrt(jnp.mean(xf * xf, -1, keepdims=True) + RMS_EPS) * w).astype(x.dtype)


def _l2norm(x):
    xf = x.astype(jnp.float32)
    return xf * lax.rsqrt(jnp.sum(xf * xf, -1, keepdims=True) + RMS_EPS)


def _causal_dwconv(x, w):
    K, C = w.shape
    return lax.conv_general_dilated(
        x, w[:, None, :].astype(x.dtype), window_strides=(1,), padding=[(K - 1, 0)],
        dimension_numbers=("NWC", "WIO", "NWC"), feature_group_count=C)


def _rope(x, cos, sin):
    x1, x2 = jnp.split(x, 2, axis=-1)
    return jnp.concatenate([x1 * cos - x2 * sin, x1 * sin + x2 * cos], axis=-1)


def _ssd_mixer(u, in_w, conv_w, conv_b, dt_bias, a_log, d_skip, norm_w, out_w):
    f32 = jnp.float32
    Bsz, S, _ = u.shape
    G, N, H, P, L = SSD_N_GROUPS, SSD_D_STATE, SSD_N_HEADS, SSD_HEAD_DIM, SSD_CHUNK
    R = H // G
    nc = S // L
    proj = u @ in_w
    z, xbc, dt = jnp.split(proj, [SSD_D_INNER, SSD_D_INNER + SSD_CONV_DIM], axis=-1)
    xbc = jax.nn.silu(_causal_dwconv(xbc, conv_w) + conv_b)
    xs, Bm, Cm = jnp.split(xbc, [SSD_D_INNER, SSD_D_INNER + G * N], axis=-1)
    dt = jax.nn.softplus(dt.astype(f32) + dt_bias.astype(f32))
    A = -jnp.exp(a_log.astype(f32)).reshape(G, R)
    x_c = xs.astype(f32).reshape(Bsz, nc, L, G, R, P)
    B_c = Bm.astype(f32).reshape(Bsz, nc, L, G, N)
    C_c = Cm.astype(f32).reshape(Bsz, nc, L, G, N)
    dt_c = dt.reshape(Bsz, nc, L, G, R)
    xdt = x_c * dt_c[..., None]
    a_cs = jnp.moveaxis(jnp.cumsum(dt_c * A, axis=2), 2, -1)
    causal = jnp.tril(jnp.ones((L, L), dtype=bool))
    decay = jnp.exp(jnp.where(causal, a_cs[..., :, None] - a_cs[..., None, :], -jnp.inf))
    cb = jnp.einsum("bclgn,bcsgn->bcgls", C_c, B_c)
    y_diag = jnp.einsum("bcgrls,bcsgrp->bclgrp", cb[:, :, :, None] * decay, xdt)
    decay_states = jnp.moveaxis(jnp.exp(a_cs[..., -1:] - a_cs), -1, 2)
    states = jnp.einsum("bcsgn,bcsgrp->bcgrpn", B_c, xdt * decay_states[..., None])
    chunk_decay = jnp.exp(a_cs[..., -1])

    def step(h, inp):
        st, dec = inp
        return h * dec[..., None, None] + st, h

    h0 = jnp.zeros((Bsz, G, R, P, N), f32)
    _, prev = lax.scan(step, h0, (jnp.moveaxis(states, 1, 0), jnp.moveaxis(chunk_decay, 1, 0)))
    prev = jnp.moveaxis(prev, 0, 1)
    decay_out = jnp.moveaxis(jnp.exp(a_cs), -1, 2)
    y_off = jnp.einsum("bclgn,bcgrpn->bclgrp", C_c, prev) * decay_out[..., None]
    y = y_diag + y_off + x_c * d_skip.astype(f32).reshape(G, R)[..., None]
    gs = SSD_D_INNER // G
    yg = y.reshape(Bsz, S, G, gs) * jax.nn.silu(z.astype(f32).reshape(Bsz, S, G, gs))
    yg = yg * lax.rsqrt(jnp.mean(yg * yg, -1, keepdims=True) + RMS_EPS) * norm_w.astype(f32).reshape(G, gs)
    return yg.reshape(Bsz, S, SSD_D_INNER).astype(u.dtype) @ out_w


def _mla_mixer(u, positions, in_w, q_norm_w, q_up_w, kv_norm_w, kv_up_w, out_w):
    f32 = jnp.float32
    Bsz, S, _ = u.shape
    H, QB = MLA_N_HEADS, MLA_Q_BLOCK
    nb = S // QB
    proj = u @ in_w
    q_c, kv_c, k_rope, z = jnp.split(
        proj, [MLA_Q_RANK, MLA_Q_RANK + MLA_KV_RANK, MLA_Q_RANK + MLA_KV_RANK + MLA_ROPE], axis=-1)
    q = (_rmsnorm(q_c, q_norm_w) @ q_up_w).astype(f32).reshape(Bsz, S, H, MLA_NOPE + MLA_ROPE)
    kv = (_rmsnorm(kv_c, kv_norm_w) @ kv_up_w).astype(f32).reshape(Bsz, S, H, MLA_NOPE + MLA_V)
    q_nope, q_rope = q[..., :MLA_NOPE], q[..., MLA_NOPE:]
    k_nope, v = kv[..., :MLA_NOPE], kv[..., MLA_NOPE:]
    inv_freq = ROPE_THETA ** (-jnp.arange(0, MLA_ROPE, 2, dtype=f32) / MLA_ROPE)
    ang = positions.astype(f32)[..., None] * inv_freq
    cos, sin = jnp.cos(ang), jnp.sin(ang)
    q_rope = _rope(q_rope, cos[:, :, None], sin[:, :, None])
    k_rope = _rope(k_rope.astype(f32), cos, sin)
    scale = (MLA_NOPE + MLA_ROPE) ** -0.5
    qn_b = q_nope.reshape(Bsz, nb, QB, H, MLA_NOPE).transpose(1, 0, 3, 2, 4)
    qr_b = q_rope.reshape(Bsz, nb, QB, H, MLA_ROPE).transpose(1, 0, 3, 2, 4)
    kpos = jnp.arange(S)

    def attend(args):
        qn, qr, blk = args
        s = (jnp.einsum("bhqd,bkhd->bhqk", qn, k_nope)
             + jnp.einsum("bhqr,bkr->bhqk", qr, k_rope)) * scale
        qpos = blk * QB + jnp.arange(QB)
        s = jnp.where(kpos[None, :] <= qpos[:, None], s, -jnp.inf)
        p = jax.nn.softmax(s, axis=-1)
        return jnp.einsum("bhqk,bkhd->bqhd", p, v)

    o = lax.map(attend, (qn_b, qr_b, jnp.arange(nb)))
    o = o.transpose(1, 0, 2, 3, 4).reshape(Bsz, S, H * MLA_V)
    o = o * jax.nn.silu(z.astype(f32))
    return o.astype(u.dtype) @ out_w


def _chunk_gated_delta(q, k, v, g, beta):
    Bsz, S, H, DK = q.shape
    DV = v.shape[-1]
    L = GDN_CHUNK
    nc = S // L

    def chunks(t):
        return jnp.swapaxes(t.reshape(Bsz, nc, L, H, *t.shape[3:]), 2, 3)

    q, k, v, g, beta = chunks(q), chunks(k), chunks(v), chunks(g), chunks(beta)
    g_cs = jnp.cumsum(g, axis=-1)
    incl = jnp.tril(jnp.ones((L, L), dtype=bool))
    strict = jnp.tril(jnp.ones((L, L), dtype=bool), -1)
    decay = jnp.exp(jnp.where(incl, g_cs[..., :, None] - g_cs[..., None, :], -jnp.inf))
    k_beta = k * beta[..., None]
    v_beta = v * beta[..., None]
    a_mat = jnp.where(strict, jnp.einsum("bchld,bchsd->bchls", k_beta, k) * decay, 0.0)
    eye = jnp.eye(L, dtype=q.dtype)
    rhs = jnp.concatenate([v_beta, k_beta * jnp.exp(g_cs)[..., None]], axis=-1)
    sol = lax.linalg.triangular_solve(eye + a_mat, rhs, left_side=True, lower=True, unit_diagonal=True)
    u_val, w_dec = sol[..., :DV], sol[..., DV:]
    qk = jnp.where(incl, jnp.einsum("bchld,bchsd->bchls", q, k) * decay, 0.0)
    q_dec = q * jnp.exp(g_cs)[..., None]
    g_last = g_cs[..., -1]
    k_dec = k * jnp.exp(g_last[..., None] - g_cs)[..., None]

    def step(state, inp):
        qk_c, q_dec_c, u_c, w_c, k_dec_c, gl_c = inp
        v_new = u_c - jnp.einsum("bhld,bhdv->bhlv", w_c, state)
        o_c = jnp.einsum("bhld,bhdv->bhlv", q_dec_c, state) + jnp.einsum("bhls,bhsv->bhlv", qk_c, v_new)
        state = state * jnp.exp(gl_c)[..., None, None] + jnp.einsum("bhld,bhlv->bhdv", k_dec_c, v_new)
        return state, o_c

    xs = tuple(jnp.moveaxis(t, 1, 0) for t in (qk, q_dec, u_val, w_dec, k_dec, g_last))
    _, o = lax.scan(step, jnp.zeros((Bsz, H, DK, DV), q.dtype), xs)
    o = jnp.swapaxes(jnp.moveaxis(o, 0, 1), 2, 3)
    return o.reshape(Bsz, S, H, DV)


def _gdn_mixer(u, in_w, conv_w, a_log, dt_bias, norm_w, out_w):
    f32 = jnp.float32
    Bsz, S, _ = u.shape
    HK, HV, DK, DV = GDN_N_QK_HEADS, GDN_N_V_HEADS, GDN_DK, GDN_DV
    proj = u @ in_w
    qkv, z, b, a = jnp.split(
        proj, [GDN_CONV_DIM, GDN_CONV_DIM + GDN_VAL_DIM, GDN_CONV_DIM + GDN_VAL_DIM + HV], axis=-1)
    qkv = jax.nn.silu(_causal_dwconv(qkv, conv_w))
    q, k, v = jnp.split(qkv, [GDN_KEY_DIM, 2 * GDN_KEY_DIM], axis=-1)
    rep = HV // HK
    q = jnp.repeat(_l2norm(q.reshape(Bsz, S, HK, DK)), rep, axis=2) * (DK ** -0.5)
    k = jnp.repeat(_l2norm(k.reshape(Bsz, S, HK, DK)), rep, axis=2)
    v = v.astype(f32).reshape(Bsz, S, HV, DV)
    beta = jax.nn.sigmoid(b.astype(f32))
    g = -jnp.exp(a_log.astype(f32)) * jax.nn.softplus(a.astype(f32) + dt_bias.astype(f32))
    o = _chunk_gated_delta(q, k, v, g, beta)
    o = _rmsnorm(o, norm_w) * jax.nn.silu(z.astype(f32).reshape(Bsz, S, HV, DV))
    return o.reshape(Bsz, S, GDN_VAL_DIM).astype(u.dtype) @ out_w


def _dt_bias(key, shape):
    dt = jnp.exp(jax.random.uniform(key, shape, jnp.float32, math.log(1e-3), math.log(1e-1)))
    return dt + jnp.log(-jnp.expm1(-dt))


def setup_inputs(seed: int = 0) -> dict:
    key = jax.random.key(seed)
    ks = jax.random.split(key, 32)
    f32 = jnp.float32

    def nrm(k, shape, scale):
        return jax.random.normal(k, shape, f32) * scale

    def gain(k, shape):
        return 1.0 + 0.02 * jax.random.normal(k, shape, f32)

    nA, nB, nC = N_SSD_LAYERS, N_MLA_LAYERS, N_GDN_LAYERS
    x = jax.random.normal(ks[0], (BATCH, SEQ, D_MODEL), f32)
    offset = jax.random.randint(ks[1], (BATCH, 1), 0, 1024, dtype=jnp.int32)
    positions = offset + jnp.arange(SEQ, dtype=jnp.int32)[None, :]
    return {
        "x": x,
        "positions": positions,
        "ssd_in_w": nrm(ks[2], (nA, D_MODEL, SSD_PROJ), D_MODEL ** -0.5),
        "ssd_conv_w": nrm(ks[3], (nA, SSD_CONV, SSD_CONV_DIM), SSD_CONV ** -0.5),
        "ssd_conv_b": nrm(ks[4], (nA, SSD_CONV_DIM), 0.02),
        "ssd_dt_bias": _dt_bias(ks[5], (nA, SSD_N_HEADS)),
        "ssd_a_log": jnp.log(jax.random.uniform(ks[6], (nA, SSD_N_HEADS), f32, 1.0, 16.0)),
        "ssd_d": gain(ks[7], (nA, SSD_N_HEADS)),
        "ssd_norm_w": gain(ks[8], (nA, SSD_D_INNER)),
        "ssd_out_w": nrm(ks[9], (nA, SSD_D_INNER, D_MODEL), SSD_D_INNER ** -0.5 * DEEPNORM_BETA),
        "mla_in_w": nrm(ks[10], (nB, D_MODEL, MLA_PROJ), D_MODEL ** -0.5),
        "mla_q_norm_w": gain(ks[11], (nB, MLA_Q_RANK)),
        "mla_q_up_w": nrm(ks[12], (nB, MLA_Q_RANK, MLA_N_HEADS * (MLA_NOPE + MLA_ROPE)), MLA_Q_RANK ** -0.5),
        "mla_kv_norm_w": gain(ks[13], (nB, MLA_KV_RANK)),
        "mla_kv_up_w": nrm(ks[14], (nB, MLA_KV_RANK, MLA_N_HEADS * (MLA_NOPE + MLA_V)), MLA_KV_RANK ** -0.5),
        "mla_out_w": nrm(ks[15], (nB, MLA_N_HEADS * MLA_V, D_MODEL), (MLA_N_HEADS * MLA_V) ** -0.5 * DEEPNORM_BETA),
        "gdn_in_w": nrm(ks[16], (nC, D_MODEL, GDN_PROJ), D_MODEL ** -0.5),
        "gdn_conv_w": nrm(ks[17], (nC, GDN_CONV, GDN_CONV_DIM), GDN_CONV ** -0.5),
        "gdn_a_log": jnp.log(jax.random.uniform(ks[18], (nC, GDN_N_V_HEADS), f32, 1.0, 16.0)),
        "gdn_dt_bias": _dt_bias(ks[19], (nC, GDN_N_V_HEADS)),
        "gdn_norm_w": gain(ks[20], (nC, GDN_DV)),
        "gdn_out_w": nrm(ks[21], (nC, GDN_VAL_DIM, D_MODEL), GDN_VAL_DIM ** -0.5 * DEEPNORM_BETA),
        "ln_g": gain(ks[22], (DEPTH, D_MODEL)),
        "ln_b": nrm(ks[23], (DEPTH, D_MODEL), 0.02),
    }


def reference(x, positions, ssd_in_w, ssd_conv_w, ssd_conv_b, ssd_dt_bias, ssd_a_log, ssd_d,
              ssd_norm_w, ssd_out_w, mla_in_w, mla_q_norm_w, mla_q_up_w, mla_kv_norm_w, mla_kv_up_w,
              mla_out_w, gdn_in_w, gdn_conv_w, gdn_a_log, gdn_dt_bias, gdn_norm_w, gdn_out_w, ln_g, ln_b):
    h = x
    for i in range(DEPTH):
        kind, j = i % N_MIXERS, i // N_MIXERS
        if kind == 0:
            y = _ssd_mixer(h, ssd_in_w[j], ssd_conv_w[j], ssd_conv_b[j], ssd_dt_bias[j], ssd_a_log[j],
                           ssd_d[j], ssd_norm_w[j], ssd_out_w[j])
        elif kind == 1:
            y = _mla_mixer(h, positions, mla_in_w[j], mla_q_norm_w[j], mla_q_up_w[j], mla_kv_norm_w[j],
                           mla_kv_up_w[j], mla_out_w[j])
        else:
            y = _gdn_mixer(h, gdn_in_w[j], gdn_conv_w[j], gdn_a_log[j], gdn_dt_bias[j], gdn_norm_w[j],
                           gdn_out_w[j])
        h = _layernorm(DEEPNORM_ALPHA * h + y.astype(h.dtype), ln_g[i], ln_b[i])
    return h
```

```python
import functools
import math

import jax
import jax.numpy as jnp
from jax import lax
from jax.experimental import pallas as pl
from jax.experimental.pallas import tpu as pltpu

F32 = jnp.float32
BF16 = jnp.bfloat16

D_MODEL = 2048
DEPTH = 4
DEEPNORM_ALPHA = (2.0 * DEPTH) ** 0.25
LN_EPS = 1e-5
RMS_EPS = 1e-6

SSD_D_INNER = 4096
SSD_HEAD_DIM = 64
SSD_N_HEADS = 64
SSD_N_GROUPS = 8
SSD_HEADS_PER_GROUP = SSD_N_HEADS // SSD_N_GROUPS
SSD_D_STATE = 128
SSD_CONV = 4
SSD_CHUNK = 128
SSD_GROUP_CH = SSD_D_INNER // SSD_N_GROUPS
SSD_BC = SSD_N_GROUPS * SSD_D_STATE
SSD_CONV_DIM = SSD_D_INNER + 2 * SSD_BC

MLA_N_HEADS = 16
MLA_Q_RANK = 768
MLA_KV_RANK = 512
MLA_NOPE = 128
MLA_ROPE = 64
MLA_V = 128
MLA_QK = MLA_NOPE + MLA_ROPE
MLA_GATE = MLA_N_HEADS * MLA_V
ROPE_THETA = 10000.0

GDN_N_QK_HEADS = 16
GDN_N_V_HEADS = 32
GDN_DK = 128
GDN_DV = 128
GDN_KEY_DIM = GDN_N_QK_HEADS * GDN_DK
GDN_VAL_DIM = GDN_N_V_HEADS * GDN_DV
GDN_CONV = 4
GDN_CHUNK = 64
GDN_CONV_DIM = 2 * GDN_KEY_DIM + GDN_VAL_DIM

LANES = 128
SUBLANES = 8
VMEM_LIMIT_BYTES = 56 * 1024 * 1024


def _cparams(sem):
    return pltpu.CompilerParams(dimension_semantics=sem, vmem_limit_bytes=VMEM_LIMIT_BYTES)


def _sigmoid(x):
    return 1.0 / (1.0 + jnp.exp(-x))


def _silu(x):
    return x * _sigmoid(x)


def _softplus(x):
    return jnp.maximum(x, 0.0) + jnp.log1p(jnp.exp(-jnp.abs(x)))


def _dot(a, b):
    return jnp.dot(a, b, preferred_element_type=F32)


def _dot_nt(a, b):
    return lax.dot_general(a, b, (((1,), (1,)), ((), ())), preferred_element_type=F32)


def _dot_tn(a, b):
    return lax.dot_general(a, b, (((0,), (0,)), ((), ())), preferred_element_type=F32)


def _dot_sel(sel_bf16, a):
    hi = a.astype(BF16)
    r1 = a - hi.astype(F32)
    mid = r1.astype(BF16)
    lo = (r1 - mid.astype(F32)).astype(BF16)
    return _dot(sel_bf16, hi) + _dot(sel_bf16, mid) + _dot(sel_bf16, lo)


def _mm_kernel(x_ref, w_ref, o_ref):
    o_ref[...] = _dot(x_ref[...], w_ref[...]).astype(o_ref.dtype)


def _pick_tile(n, pref):
    t = min(n, pref)
    while n % t:
        t -= LANES
    return t


def _matmul(x, w, name, out_dtype=F32, tm=1024, tn=512):
    T, K = x.shape
    N = w.shape[1]
    tm = min(tm, T)
    tn = _pick_tile(N, tn)
    return pl.pallas_call(
        _mm_kernel,
        grid=(T // tm, N // tn),
        in_specs=[pl.BlockSpec((tm, K), lambda i, j: (i, 0)),
                  pl.BlockSpec((K, tn), lambda i, j: (0, j))],
        out_specs=pl.BlockSpec((tm, tn), lambda i, j: (i, j)),
        out_shape=jax.ShapeDtypeStruct((T, N), out_dtype),
        compiler_params=_cparams(("parallel", "parallel")),
        name=name,
    )(x, w)


def _outproj_ln_kernel(x_ref, w_ref, h_ref, g_ref, b_ref, o_ref, ob_ref, acc_ref):
    k = pl.program_id(1)

    @pl.when(k == 0)
    def _():
        acc_ref[...] = jnp.zeros_like(acc_ref)

    acc_ref[...] += _dot(x_ref[...], w_ref[...])

    @pl.when(k == pl.num_programs(1) - 1)
    def _():
        r = DEEPNORM_ALPHA * h_ref[...] + acc_ref[...]
        mu = jnp.mean(r, axis=-1, keepdims=True)
        d = r - mu
        var = jnp.mean(d * d, axis=-1, keepdims=True)
        out = d * lax.rsqrt(var + LN_EPS) * g_ref[...] + b_ref[...]
        o_ref[...] = out
        ob_ref[...] = out.astype(BF16)


def _outproj_ln(x, w, h, g, b, name, tm=512, tk=512):
    T, K = x.shape
    N = w.shape[1]
    tm = min(tm, T)
    tk = min(tk, K)
    return pl.pallas_call(
        _outproj_ln_kernel,
        grid=(T // tm, K // tk),
        in_specs=[pl.BlockSpec((tm, tk), lambda i, k: (i, k)),
                  pl.BlockSpec((tk, N), lambda i, k: (k, 0)),
                  pl.BlockSpec((tm, N), lambda i, k: (i, 0)),
                  pl.BlockSpec((1, N), lambda i, k: (0, 0)),
                  pl.BlockSpec((1, N), lambda i, k: (0, 0))],
        out_specs=[pl.BlockSpec((tm, N), lambda i, k: (i, 0)),
                   pl.BlockSpec((tm, N), lambda i, k: (i, 0))],
        out_shape=[jax.ShapeDtypeStruct((T, N), F32), jax.ShapeDtypeStruct((T, N), BF16)],
        scratch_shapes=[pltpu.VMEM((tm, N), F32)],
        compiler_params=_cparams(("parallel", "arbitrary")),
        name=name,
    )(x, w, h, g.reshape(1, N), b.reshape(1, N))


def _conv_kernel(x_ref, p_ref, w_ref, b_ref, o_ref, *, taps):
    i = pl.program_id(1)
    cur = x_ref[0]
    ts = cur.shape[0]
    prev = jnp.where(i > 0, p_ref[0], 0.0)
    xc = jnp.concatenate([prev, cur], axis=0)
    acc = jnp.broadcast_to(b_ref[...], cur.shape)
    for k in range(taps):
        off = SUBLANES - (taps - 1) + k
        acc = acc + w_ref[k:k + 1, :] * xc[off:off + ts]
    o_ref[0] = _silu(acc)


def _conv_silu(src, c0, w, b, name, ts=512, tc=512):
    B, S, _ = src.shape
    taps, C = w.shape
    ts = min(ts, S)
    cb0 = c0 // tc
    assert c0 % tc == 0 and C % tc == 0
    rpb = ts // SUBLANES
    return pl.pallas_call(
        functools.partial(_conv_kernel, taps=taps),
        grid=(B, S // ts, C // tc),
        in_specs=[pl.BlockSpec((1, ts, tc), lambda bb, i, j: (bb, i, cb0 + j)),
                  pl.BlockSpec((1, SUBLANES, tc),
                               lambda bb, i, j: (bb, jnp.maximum(i * rpb - 1, 0), cb0 + j)),
                  pl.BlockSpec((taps, tc), lambda bb, i, j: (0, j)),
                  pl.BlockSpec((1, tc), lambda bb, i, j: (0, j))],
        out_specs=pl.BlockSpec((1, ts, tc), lambda bb, i, j: (bb, i, j)),
        out_shape=jax.ShapeDtypeStruct((B, S, C), F32),
        compiler_params=_cparams(("parallel", "parallel", "parallel")),
        name=name,
    )(src, src, w, b.reshape(1, C))


def _ssd_kernel(x_ref, b_ref, c_ref, z_ref, dt_ref, hp_ref, d_ref, nw_ref, o_ref,
                state_ref, y_ref):
    L, N, P = SSD_CHUNK, SSD_D_STATE, SSD_HEAD_DIM
    ci = pl.program_id(2)

    @pl.when(ci == 0)
    def _():
        state_ref[...] = jnp.zeros_like(state_ref)

    lane = lax.broadcasted_iota(jnp.int32, (1, LANES), 1)
    rows = lax.broadcasted_iota(jnp.int32, (L, L), 0)
    cols = lax.broadcasted_iota(jnp.int32, (L, L), 1)
    causal = rows >= cols
    tri = causal.astype(BF16)

    dt = _softplus(dt_ref[0] + hp_ref[0, 0:1, :])
    neg_a = jnp.where(lane < SSD_HEADS_PER_GROUP, -jnp.exp(hp_ref[0, 1:2, :]), 0.0)
    a_cs = _dot_sel(tri, dt * neg_a)
    a_cs_t = a_cs.T
    dt_t = dt.T
    a_last_t = a_cs_t[:, L - 1:L]
    state_w_t = dt_t * jnp.exp(a_last_t - a_cs_t)
    chunk_decay = jnp.exp(a_cs[L - 1:L, :])

    bg = b_ref[0]
    cg = c_ref[0]
    bg_t = bg.T
    cb = _dot_nt(cg.astype(BF16), bg.astype(BF16))

    lo_half = lane < P
    ssq = jnp.zeros((L, LANES), F32)
    for j in range(SSD_HEADS_PER_GROUP // 2):
        sl = slice(j * LANES, (j + 1) * LANES)
        xp = x_ref[0, :, sl]
        xb = xp.astype(BF16)
        zero = jnp.zeros_like(xb)
        x_lo = jnp.where(lo_half, xb, zero)
        x_hi = jnp.where(lo_half, zero, xb)
        st = state_ref[:, sl]
        sb = st.astype(BF16)
        s_lo = jnp.where(lo_half, sb, zero)
        s_hi = jnp.where(lo_half, zero, sb)
        lhs_y, lhs_s = [], []
        for e in range(2):
            h = 2 * j + e
            col = jnp.broadcast_to(a_cs[:, h:h + 1], (L, L))
            row = a_cs_t[h:h + 1, :]
            decay = jnp.exp(jnp.where(causal, col - row, -jnp.inf))
            lhs_y.append((cb * decay * dt_t[h:h + 1, :]).astype(BF16))
            lhs_y.append((cg * jnp.exp(col)).astype(BF16))
            lhs_s.append((bg_t * state_w_t[h:h + 1, :]).astype(BF16))
        y = _dot(jnp.concatenate(lhs_y, axis=1),
                 jnp.concatenate([x_lo, s_lo, x_hi, s_hi], axis=0))
        s_new = _dot(jnp.concatenate(lhs_s, axis=1),
                     jnp.concatenate([x_lo, x_hi], axis=0))
        cd = jnp.where(lo_half,
                       jnp.broadcast_to(chunk_decay[:, 2 * j:2 * j + 1], (1, LANES)),
                       jnp.broadcast_to(chunk_decay[:, 2 * j + 1:2 * j + 2], (1, LANES)))
        state_ref[:, sl] = st * cd + s_new
        y = y + xp * d_ref[:, sl]
        yg = y * _silu(z_ref[0, :, sl])
        ssq = ssq + yg * yg
        y_ref[:, sl] = yg
    ms = jnp.sum(ssq, axis=-1, keepdims=True) * (1.0 / SSD_GROUP_CH)
    scale = lax.rsqrt(ms + RMS_EPS)
    o_ref[0] = (y_ref[...] * scale * nw_ref[...]).astype(o_ref.dtype)


def _ssd_scan(xbc, proj, dtp, hp, dskip, normw, name):
    B, S, _ = xbc.shape
    L, G, GC, N = SSD_CHUNK, SSD_N_GROUPS, SSD_GROUP_CH, SSD_D_STATE
    nc = S // L
    b_blk0 = SSD_D_INNER // N
    c_blk0 = (SSD_D_INNER + SSD_BC) // N
    return pl.pallas_call(
        _ssd_kernel,
        grid=(B, G, nc),
        in_specs=[pl.BlockSpec((1, L, GC), lambda b, g, c: (b, c, g)),
                  pl.BlockSpec((1, L, N), lambda b, g, c: (b, c, b_blk0 + g)),
                  pl.BlockSpec((1, L, N), lambda b, g, c: (b, c, c_blk0 + g)),
                  pl.BlockSpec((1, L, GC), lambda b, g, c: (b, c, g)),
                  pl.BlockSpec((1, L, LANES), lambda b, g, c: (b, c, g)),
                  pl.BlockSpec((1, SUBLANES, LANES), lambda b, g, c: (g, 0, 0)),
                  pl.BlockSpec((1, GC), lambda b, g, c: (0, g)),
                  pl.BlockSpec((1, GC), lambda b, g, c: (0, g))],
        out_specs=pl.BlockSpec((1, L, GC), lambda b, g, c: (b, c, g)),
        out_shape=jax.ShapeDtypeStruct((B, S, SSD_D_INNER), BF16),
        scratch_shapes=[pltpu.VMEM((N, GC), F32), pltpu.VMEM((L, GC), F32)],
        compiler_params=_cparams(("parallel", "parallel", "arbitrary")),
        name=name,
    )(xbc, xbc, xbc, proj, dtp, hp, dskip, normw)


def _pad_heads(a, groups):
    per = a.shape[-1] // groups
    a = a.reshape(a.shape[:-1] + (groups, per))
    a = jnp.pad(a, [(0, 0)] * (a.ndim - 1) + [(0, LANES - per)])
    return a.reshape(a.shape[:-2] + (groups * LANES,))


def _ssd_layer(h, hb, in_w, conv_w, conv_b, dt_bias, a_log, d_skip, norm_w, out_w, ln_g, ln_b, tag):
    B, S, D = h.shape
    T = B * S
    n_zx = SSD_D_INNER + SSD_CONV_DIM
    w_zx = in_w[:, :n_zx].astype(BF16)
    w_dt = in_w[:, n_zx:].astype(BF16)
    proj = _matmul(hb, w_zx, f"{tag}_in").reshape(B, S, n_zx)
    dt_raw = _matmul(hb, w_dt, f"{tag}_dt").reshape(B, S, SSD_N_HEADS)
    xbc = _conv_silu(proj, SSD_D_INNER, conv_w, conv_b, f"{tag}_conv")
    dtp = _pad_heads(dt_raw, SSD_N_GROUPS)
    hp = jnp.zeros((SSD_N_GROUPS, SUBLANES, LANES), F32)
    hp = hp.at[:, 0, :SSD_HEADS_PER_GROUP].set(dt_bias.reshape(SSD_N_GROUPS, -1))
    hp = hp.at[:, 1, :SSD_HEADS_PER_GROUP].set(a_log.reshape(SSD_N_GROUPS, -1))
    dskip = jnp.repeat(d_skip, SSD_HEAD_DIM).reshape(1, SSD_D_INNER)
    y = _ssd_scan(xbc, proj, dtp, hp, dskip, norm_w.reshape(1, SSD_D_INNER), f"{tag}_scan")
    hn, hnb = _outproj_ln(y.reshape(T, SSD_D_INNER), out_w.astype(BF16), h.reshape(T, D),
                          ln_g, ln_b, f"{tag}_out")
    return hn.reshape(B, S, D), hnb


def _rope_tables(pos_ref, invf_ref, cc_ref, ss_ref):
    lane = lax.broadcasted_iota(jnp.int32, (1, LANES), 1)
    ang = pos_ref[0].astype(F32) * invf_ref[...]
    cc_ref[...] = jnp.cos(ang)
    ss_ref[...] = jnp.where(lane < MLA_ROPE // 2, -1.0, 1.0) * jnp.sin(ang)


def _mla_q_kernel(qc_ref, nw_ref, w_ref, pos_ref, invf_ref, o_ref, xn_ref, cc_ref, ss_ref):
    hd = pl.program_id(2)

    @pl.when(hd == 0)
    def _():
        x = qc_ref[0]
        ms = jnp.mean(x * x, axis=-1, keepdims=True)
        xn_ref[...] = (x * lax.rsqrt(ms + RMS_EPS) * nw_ref[...]).astype(BF16)
        _rope_tables(pos_ref, invf_ref, cc_ref, ss_ref)

    r = _dot(xn_ref[...], w_ref[0])
    scale = MLA_QK ** -0.5
    roped = r[:, LANES:2 * LANES] * cc_ref[...] + r[:, 2 * LANES:] * ss_ref[...]
    o_ref[0, 0, :, :MLA_NOPE] = (r[:, :MLA_NOPE] * scale).astype(BF16)
    o_ref[0, 0, :, MLA_NOPE:] = (roped[:, :MLA_ROPE] * scale).astype(BF16)


def _mla_kv_kernel(kvc_ref, nw_ref, w_ref, kr_ref, pos_ref, invf_ref, k_ref, v_ref,
                   xn_ref, kr_s):
    hd = pl.program_id(2)

    @pl.when(hd == 0)
    def _():
        x = kvc_ref[0]
        ms = jnp.mean(x * x, axis=-1, keepdims=True)
        xn_ref[...] = (x * lax.rsqrt(ms + RMS_EPS) * nw_ref[...]).astype(BF16)
        lane = lax.broadcasted_iota(jnp.int32, (1, LANES), 1)
        ang = pos_ref[0].astype(F32) * invf_ref[...]
        cc = jnp.cos(ang)
        ss = jnp.where(lane < MLA_ROPE // 2, -1.0, 1.0) * jnp.sin(ang)
        kr_s[...] = kr_ref[0, :, :LANES] * cc + kr_ref[0, :, LANES:] * ss

    r = _dot(xn_ref[...], w_ref[0])
    k_ref[0, 0, :, :MLA_NOPE] = r[:, :MLA_NOPE].astype(BF16)
    k_ref[0, 0, :, MLA_NOPE:] = kr_s[:, :MLA_ROPE].astype(BF16)
    v_ref[0, 0] = r[:, MLA_NOPE:].astype(BF16)


def _mla_attn_kernel(q_ref, k_ref, v_ref, z_ref, o_ref, *, tq):
    qi = pl.program_id(2)
    q = q_ref[0, 0]

    def block(kb, carry, masked):
        m, l, acc = carry
        start = pl.multiple_of(kb * tq, tq)
        k = k_ref[0, 0, pl.ds(start, tq), :]
        v = v_ref[0, 0, pl.ds(start, tq), :]
        s = _dot_nt(q, k)
        if masked:
            rows = lax.broadcasted_iota(jnp.int32, (tq, tq), 0)
            cols = lax.broadcasted_iota(jnp.int32, (tq, tq), 1)
            s = jnp.where(rows >= cols, s, -jnp.inf)
        m_new = jnp.maximum(m, jnp.max(s, axis=-1, keepdims=True))
        p = jnp.exp(s - m_new)
        alpha = jnp.exp(m - m_new)
        l = alpha * l + jnp.sum(p, axis=-1, keepdims=True)
        acc = alpha * acc + _dot(p.astype(BF16), v)
        return m_new, l, acc

    init = (jnp.full((tq, 1), -jnp.inf, F32), jnp.zeros((tq, 1), F32), jnp.zeros((tq, MLA_V), F32))
    carry = lax.fori_loop(0, qi, lambda kb, c: block(kb, c, False), init)
    m, l, acc = block(qi, carry, True)
    o = acc / l
    o_ref[0] = (o * _silu(z_ref[0])).astype(o_ref.dtype)


def _mla_layer(h, hb, positions, in_w, q_norm_w, q_up_w, kv_norm_w, kv_up_w, out_w, ln_g, ln_b, tag):
    B, S, D = h.shape
    T = B * S
    H = MLA_N_HEADS
    o_kv = MLA_Q_RANK
    o_kr = o_kv + MLA_KV_RANK
    o_z = o_kr + MLA_ROPE
    half = MLA_ROPE // 2
    w_kr = in_w[:, o_kr:o_z]
    w_kr_sw = jnp.concatenate([w_kr[:, half:], w_kr[:, :half]], axis=1)
    zpad = jnp.zeros((D, LANES - MLA_ROPE), F32)
    w_kr2 = jnp.concatenate([w_kr, zpad, w_kr_sw, zpad], axis=1)
    q_c = _matmul(hb, in_w[:, :o_kv].astype(BF16), f"{tag}_qc", tn=768).reshape(B, S, MLA_Q_RANK)
    kv_c = _matmul(hb, in_w[:, o_kv:o_kr].astype(BF16), f"{tag}_kvc").reshape(B, S, MLA_KV_RANK)
    kr = _matmul(hb, w_kr2.astype(BF16), f"{tag}_kr").reshape(B, S, 2 * LANES)
    z = _matmul(hb, in_w[:, o_z:].astype(BF16), f"{tag}_z").reshape(B, S, MLA_GATE)

    wq = q_up_w.reshape(MLA_Q_RANK, H, MLA_QK)
    wq_n, wq_r = wq[:, :, :MLA_NOPE], wq[:, :, MLA_NOPE:]
    wq_sw = jnp.concatenate([wq_r[:, :, half:], wq_r[:, :, :half]], axis=2)
    zq = jnp.zeros((MLA_Q_RANK, H, LANES - MLA_ROPE), F32)
    wq3 = jnp.concatenate([wq_n, wq_r, zq, wq_sw, zq], axis=2)
    wq3 = jnp.transpose(wq3, (1, 0, 2)).astype(BF16)
    wkv = jnp.transpose(kv_up_w.reshape(MLA_KV_RANK, H, MLA_NOPE + MLA_V), (1, 0, 2)).astype(BF16)

    inv_freq = ROPE_THETA ** (-jnp.arange(0, MLA_ROPE, 2, dtype=F32) / MLA_ROPE)
    invf = jnp.concatenate([inv_freq, inv_freq, jnp.zeros((LANES - MLA_ROPE,), F32)]).reshape(1, LANES)
    pos3 = positions.reshape(B, S, 1)

    tm = min(512, S)
    qf = pl.pallas_call(
        _mla_q_kernel,
        grid=(B, S // tm, H),
        in_specs=[pl.BlockSpec((1, tm, MLA_Q_RANK), lambda b, i, hd: (b, i, 0)),
                  pl.BlockSpec((1, MLA_Q_RANK), lambda b, i, hd: (0, 0)),
                  pl.BlockSpec((1, MLA_Q_RANK, 3 * LANES), lambda b, i, hd: (hd, 0, 0)),
                  pl.BlockSpec((1, tm, 1), lambda b, i, hd: (b, i, 0)),
                  pl.BlockSpec((1, LANES), lambda b, i, hd: (0, 0))],
        out_specs=pl.BlockSpec((1, 1, tm, MLA_QK), lambda b, i, hd: (b, hd, i, 0)),
        out_shape=jax.ShapeDtypeStruct((B, H, S, MLA_QK), BF16),
        scratch_shapes=[pltpu.VMEM((tm, MLA_Q_RANK), BF16), pltpu.VMEM((tm, LANES), F32),
                        pltpu.VMEM((tm, LANES), F32)],
        compiler_params=_cparams(("parallel", "parallel", "arbitrary")),
        name=f"{tag}_qup",
    )(q_c, q_norm_w.reshape(1, -1), wq3, pos3, invf)

    kf, vf = pl.pallas_call(
        _mla_kv_kernel,
        grid=(B, S // tm, H),
        in_specs=[pl.BlockSpec((1, tm, MLA_KV_RANK), lambda b, i, hd: (b, i, 0)),
                  pl.BlockSpec((1, MLA_KV_RANK), lambda b, i, hd: (0, 0)),
                  pl.BlockSpec((1, MLA_KV_RANK, MLA_NOPE + MLA_V), lambda b, i, hd: (hd, 0, 0)),
                  pl.BlockSpec((1, tm, 2 * LANES), lambda b, i, hd: (b, i, 0)),
                  pl.BlockSpec((1, tm, 1), lambda b, i, hd: (b, i, 0)),
                  pl.BlockSpec((1, LANES), lambda b, i, hd: (0, 0))],
        out_specs=[pl.BlockSpec((1, 1, tm, MLA_QK), lambda b, i, hd: (b, hd, i, 0)),
                   pl.BlockSpec((1, 1, tm, MLA_V), lambda b, i, hd: (b, hd, i, 0))],
        out_shape=[jax.ShapeDtypeStruct((B, H, S, MLA_QK), BF16),
                   jax.ShapeDtypeStruct((B, H, S, MLA_V), BF16)],
        scratch_shapes=[pltpu.VMEM((tm, MLA_KV_RANK), BF16), pltpu.VMEM((tm, LANES), F32)],
        compiler_params=_cparams(("parallel", "parallel", "arbitrary")),
        name=f"{tag}_kvup",
    )(kv_c, kv_norm_w.reshape(1, -1), wkv, kr, pos3, invf)

    tq = min(512, S)
    o = pl.pallas_call(
        functools.partial(_mla_attn_kernel, tq=tq),
        grid=(B, H, S // tq),
        in_specs=[pl.BlockSpec((1, 1, tq, MLA_QK), lambda b, hd, i: (b, hd, i, 0)),
                  pl.BlockSpec((1, 1, S, MLA_QK), lambda b, hd, i: (b, hd, 0, 0)),
                  pl.BlockSpec((1, 1, S, MLA_V), lambda b, hd, i: (b, hd, 0, 0)),
                  pl.BlockSpec((1, tq, MLA_V), lambda b, hd, i: (b, i, hd))],
        out_specs=pl.BlockSpec((1, tq, MLA_V), lambda b, hd, i: (b, i, hd)),
        out_shape=jax.ShapeDtypeStruct((B, S, MLA_GATE), BF16),
        compiler_params=_cparams(("parallel", "parallel", "arbitrary")),
        name=f"{tag}_attn",
    )(qf, kf, vf, z)

    hn, hnb = _outproj_ln(o.reshape(T, MLA_GATE), out_w.astype(BF16), h.reshape(T, D),
                          ln_g, ln_b, f"{tag}_out")
    return hn.reshape(B, S, D), hnb


GDN_QK_PER_STEP = 4
GDN_V_PER_STEP = 2 * GDN_QK_PER_STEP
GDN_CHUNKS_PER_STEP = 4
GDN_ROWS = GDN_CHUNKS_PER_STEP * GDN_CHUNK


def _unit_lower_inverse(a):
    n = a.shape[0]
    rows = lax.broadcasted_iota(jnp.int32, (n, n), 0)
    cols = lax.broadcasted_iota(jnp.int32, (n, n), 1)
    t = jnp.where(rows == cols, 1.0, 0.0) - a
    p = a
    steps = int(math.log2(n)) - 1
    for i in range(steps):
        pb = p.astype(BF16)
        p = _dot(pb, pb)
        t = t + _dot(t.astype(BF16), p.astype(BF16))
    return t


def _gdn_kernel(q_ref, k_ref, v_ref, z_ref, b_ref, a_ref, hp_ref, nw_ref, o_ref,
                state_ref, gcs_ref, beta_ref, gt_ref, bt_ref):
    L, R, C = GDN_CHUNK, GDN_ROWS, GDN_CHUNKS_PER_STEP
    ri = pl.program_id(2)

    @pl.when(ri == 0)
    def _():
        state_ref[...] = jnp.zeros_like(state_ref)

    beta = _sigmoid(b_ref[0])
    g = -jnp.exp(hp_ref[0, 0:1, :]) * _softplus(a_ref[0] + hp_ref[0, 1:2, :])
    rr = lax.broadcasted_iota(jnp.int32, (R, R), 0)
    cc = lax.broadcasted_iota(jnp.int32, (R, R), 1)
    blocktri = ((rr >= cc) & (rr // L == cc // L)).astype(BF16)
    gcs = _dot_sel(blocktri, g)
    gcs_ref[...] = gcs
    beta_ref[...] = beta
    gcs_t = gcs.T
    beta_t = beta.T
    for c in range(C):
        gt_ref[c] = gcs_t[:, c * L:(c + 1) * L]
        bt_ref[c] = beta_t[:, c * L:(c + 1) * L]

    rows = lax.broadcasted_iota(jnp.int32, (L, L), 0)
    cols = lax.broadcasted_iota(jnp.int32, (L, L), 1)
    incl = rows >= cols
    strict = rows > cols
    nw = nw_ref[...]

    def chunk(c, carry):
        r0 = pl.multiple_of(c * L, L)
        sl = pl.ds(r0, L)
        for i in range(GDN_QK_PER_STEP):
            q = q_ref[0, sl, i * GDN_DK:(i + 1) * GDN_DK]
            k = k_ref[0, sl, i * GDN_DK:(i + 1) * GDN_DK]
            qn = q * lax.rsqrt(jnp.sum(q * q, axis=-1, keepdims=True) + RMS_EPS) * (GDN_DK ** -0.5)
            kn = k * lax.rsqrt(jnp.sum(k * k, axis=-1, keepdims=True) + RMS_EPS)
            qb = qn.astype(BF16)
            kb = kn.astype(BF16)
            kk = _dot_nt(kb, kb)
            qk = _dot_nt(qb, kb)
            for e in range(2):
                hv = 2 * i + e
                gc = gcs_ref[sl, hv:hv + 1]
                gr = gt_ref[c, hv:hv + 1, :]
                bc = beta_ref[sl, hv:hv + 1]
                br = bt_ref[c, hv:hv + 1, :]
                decay = jnp.exp(jnp.where(incl, gc - gr, -jnp.inf))
                a_mat = jnp.where(strict, kk * bc * decay, 0.0)
                t = _unit_lower_inverse(a_mat)
                tb = t * br
                tbg = tb * jnp.exp(gr)
                v = v_ref[0, sl, hv * GDN_DV:(hv + 1) * GDN_DV]
                u = _dot(tb.astype(BF16), v.astype(BF16))
                w = _dot(tbg.astype(BF16), kb)
                st = state_ref[hv]
                sb = st.astype(BF16)
                v_new = u - _dot(w.astype(BF16), sb)
                vb = v_new.astype(BF16)
                qd = (qn * jnp.exp(gc)).astype(BF16)
                o = _dot(jnp.concatenate([(qk * decay).astype(BF16), qd], axis=1),
                         jnp.concatenate([vb, sb], axis=0))
                gl = gcs_ref[pl.ds(r0 + L - 1, 1), hv:hv + 1]
                kd = (kn * jnp.exp(gl - gc)).astype(BF16)
                state_ref[hv] = st * jnp.exp(gl) + _dot_tn(kd, vb)
                on = o * lax.rsqrt(jnp.mean(o * o, axis=-1, keepdims=True) + RMS_EPS) * nw
                zz = z_ref[0, sl, hv * GDN_DV:(hv + 1) * GDN_DV]
                o_ref[0, sl, hv * GDN_DV:(hv + 1) * GDN_DV] = (on * _silu(zz)).astype(o_ref.dtype)
        return carry

    lax.fori_loop(0, C, chunk, 0)


def _gdn_core(qkv, proj, bp, ap, hp, normw, name):
    B, S, _ = qkv.shape
    R = min(GDN_ROWS, S)
    assert R == GDN_ROWS
    HG = GDN_N_QK_HEADS // GDN_QK_PER_STEP
    wq = GDN_QK_PER_STEP * GDN_DK
    wv = GDN_V_PER_STEP * GDN_DV
    k_blk0 = GDN_KEY_DIM // wq
    v_blk0 = 2 * GDN_KEY_DIM // wv
    z_blk0 = GDN_CONV_DIM // wv
    return pl.pallas_call(
        _gdn_kernel,
        grid=(B, HG, S // R),
        in_specs=[pl.BlockSpec((1, R, wq), lambda b, g, r: (b, r, g)),
                  pl.BlockSpec((1, R, wq), lambda b, g, r: (b, r, k_blk0 + g)),
                  pl.BlockSpec((1, R, wv), lambda b, g, r: (b, r, v_blk0 + g)),
                  pl.BlockSpec((1, R, wv), lambda b, g, r: (b, r, z_blk0 + g)),
                  pl.BlockSpec((1, R, LANES), lambda b, g, r: (b, r, g)),
                  pl.BlockSpec((1, R, LANES), lambda b, g, r: (b, r, g)),
                  pl.BlockSpec((1, SUBLANES, LANES), lambda b, g, r: (g, 0, 0)),
                  pl.BlockSpec((1, GDN_DV), lambda b, g, r: (0, 0))],
        out_specs=pl.BlockSpec((1, R, wv), lambda b, g, r: (b, r, g)),
        out_shape=jax.ShapeDtypeStruct((B, S, GDN_VAL_DIM), BF16),
        scratch_shapes=[pltpu.VMEM((GDN_V_PER_STEP, GDN_DK, GDN_DV), F32),
                        pltpu.VMEM((R, LANES), F32), pltpu.VMEM((R, LANES), F32),
                        pltpu.VMEM((GDN_CHUNKS_PER_STEP, LANES, GDN_CHUNK), F32),
                        pltpu.VMEM((GDN_CHUNKS_PER_STEP, LANES, GDN_CHUNK), F32)],
        compiler_params=_cparams(("parallel", "parallel", "arbitrary")),
        name=name,
    )(qkv, qkv, qkv, proj, bp, ap, hp, normw)


def _gdn_layer(h, hb, in_w, conv_w, a_log, dt_bias, norm_w, out_w, ln_g, ln_b, tag):
    B, S, D = h.shape
    T = B * S
    n_qz = GDN_CONV_DIM + GDN_VAL_DIM
    HG = GDN_N_QK_HEADS // GDN_QK_PER_STEP
    proj = _matmul(hb, in_w[:, :n_qz].astype(BF16), f"{tag}_in").reshape(B, S, n_qz)
    ba = _matmul(hb, in_w[:, n_qz:].astype(BF16), f"{tag}_ba").reshape(B, S, 2 * GDN_N_V_HEADS)
    qkv = _conv_silu(proj, 0, conv_w, jnp.zeros((GDN_CONV_DIM,), F32), f"{tag}_conv")
    bp = _pad_heads(ba[..., :GDN_N_V_HEADS], HG)
    ap = _pad_heads(ba[..., GDN_N_V_HEADS:], HG)
    hp = jnp.zeros((HG, SUBLANES, LANES), F32)
    hp = hp.at[:, 0, :GDN_V_PER_STEP].set(a_log.reshape(HG, -1))
    hp = hp.at[:, 1, :GDN_V_PER_STEP].set(dt_bias.reshape(HG, -1))
    o = _gdn_core(qkv, proj, bp, ap, hp, norm_w.reshape(1, GDN_DV), f"{tag}_core")
    hn, hnb = _outproj_ln(o.reshape(T, GDN_VAL_DIM), out_w.astype(BF16), h.reshape(T, D),
                          ln_g, ln_b, f"{tag}_out")
    return hn.reshape(B, S, D), hnb


def kernel(x, positions, ssd_in_w, ssd_conv_w, ssd_conv_b, ssd_dt_bias, ssd_a_log, ssd_d, ssd_norm_w,
           ssd_out_w, mla_in_w, mla_q_norm_w, mla_q_up_w, mla_kv_norm_w, mla_kv_up_w, mla_out_w,
           gdn_in_w, gdn_conv_w, gdn_a_log, gdn_dt_bias, gdn_norm_w, gdn_out_w, ln_g, ln_b):
    B, S, D = x.shape
    h = x
    hb = x.reshape(B * S, D).astype(BF16)
    for i in range(DEPTH):
        kind, j = i % 3, i // 3
        if kind == 0:
            h, hb = _ssd_layer(h, hb, ssd_in_w[j], ssd_conv_w[j], ssd_conv_b[j], ssd_dt_bias[j],
                               ssd_a_log[j], ssd_d[j], ssd_norm_w[j], ssd_out_w[j], ln_g[i], ln_b[i],
                               f"l{i}_ssd")
        elif kind == 1:
            h, hb = _mla_layer(h, hb, positions, mla_in_w[j], mla_q_norm_w[j], mla_q_up_w[j],
                               mla_kv_norm_w[j], mla_kv_up_w[j], mla_out_w[j], ln_g[i], ln_b[i],
                               f"l{i}_mla")
        else:
            h, hb = _gdn_layer(h, hb, gdn_in_w[j], gdn_conv_w[j], gdn_a_log[j], gdn_dt_bias[j],
                               gdn_norm_w[j], gdn_out_w[j], ln_g[i], ln_b[i], f"l{i}_gdn")
    return h
```

```python
import functools
import math

import jax
import jax.numpy as jnp
from jax import lax
from jax.experimental import pallas as pl
from jax.experimental.pallas import tpu as pltpu

F32 = jnp.float32
BF16 = jnp.bfloat16

D_MODEL = 2048
DEPTH = 4
DEEPNORM_ALPHA = (2.0 * DEPTH) ** 0.25
LN_EPS = 1e-5
RMS_EPS = 1e-6

SSD_D_INNER = 4096
SSD_HEAD_DIM = 64
SSD_N_HEADS = 64
SSD_N_GROUPS = 8
SSD_HEADS_PER_GROUP = SSD_N_HEADS // SSD_N_GROUPS
SSD_D_STATE = 128
SSD_CONV = 4
SSD_CHUNK = 128
SSD_GROUP_CH = SSD_D_INNER // SSD_N_GROUPS
SSD_BC = SSD_N_GROUPS * SSD_D_STATE
SSD_CONV_DIM = SSD_D_INNER + 2 * SSD_BC

MLA_N_HEADS = 16
MLA_Q_RANK = 768
MLA_KV_RANK = 512
MLA_NOPE = 128
MLA_ROPE = 64
MLA_V = 128
MLA_QK = MLA_NOPE + MLA_ROPE
MLA_GATE = MLA_N_HEADS * MLA_V
ROPE_THETA = 10000.0

GDN_N_QK_HEADS = 16
GDN_N_V_HEADS = 32
GDN_DK = 128
GDN_DV = 128
GDN_KEY_DIM = GDN_N_QK_HEADS * GDN_DK
GDN_VAL_DIM = GDN_N_V_HEADS * GDN_DV
GDN_CONV = 4
GDN_CHUNK = 64
GDN_CONV_DIM = 2 * GDN_KEY_DIM + GDN_VAL_DIM

LANES = 128
SUBLANES = 8
VMEM_LIMIT_BYTES = 56 * 1024 * 1024


def _cparams(sem):
    return pltpu.CompilerParams(dimension_semantics=sem, vmem_limit_bytes=VMEM_LIMIT_BYTES)


def _sigmoid(x):
    return 1.0 / (1.0 + jnp.exp(-x))


def _silu(x):
    return x * _sigmoid(x)


def _softplus(x):
    return jnp.maximum(x, 0.0) + jnp.log1p(jnp.exp(-jnp.abs(x)))


def _dot(a, b):
    return jnp.dot(a, b, preferred_element_type=F32)


def _dot_nt(a, b):
    return lax.dot_general(a, b, (((1,), (1,)), ((), ())), preferred_element_type=F32)


def _dot_tn(a, b):
    return lax.dot_general(a, b, (((0,), (0,)), ((), ())), preferred_element_type=F32)


def _dot_sel(sel_bf16, a):
    hi = a.astype(BF16)
    r1 = a - hi.astype(F32)
    mid = r1.astype(BF16)
    lo = (r1 - mid.astype(F32)).astype(BF16)
    return _dot(sel_bf16, hi) + _dot(sel_bf16, mid) + _dot(sel_bf16, lo)


def _mm_kernel(x_ref, w_ref, o_ref):
    o_ref[...] = _dot(x_ref[...], w_ref[...]).astype(o_ref.dtype)


def _pick_tile(n, pref):
    t = min(n, pref)
    while n % t:
        t -= LANES
    return t


def _matmul(x, w, name, out_dtype=F32, tm=1024, tn=512):
    T, K = x.shape
    N = w.shape[1]
    tm = min(tm, T)
    tn = _pick_tile(N, tn)
    return pl.pallas_call(
        _mm_kernel,
        grid=(T // tm, N // tn),
        in_specs=[pl.BlockSpec((tm, K), lambda i, j: (i, 0)),
                  pl.BlockSpec((K, tn), lambda i, j: (0, j))],
        out_specs=pl.BlockSpec((tm, tn), lambda i, j: (i, j)),
        out_shape=jax.ShapeDtypeStruct((T, N), out_dtype),
        compiler_params=_cparams(("parallel", "parallel")),
        name=name,
    )(x, w)


def _outproj_ln_kernel(x_ref, w_ref, h_ref, g_ref, b_ref, o_ref, ob_ref, acc_ref):
    k = pl.program_id(1)

    @pl.when(k == 0)
    def _():
        acc_ref[...] = jnp.zeros_like(acc_ref)

    acc_ref[...] += _dot(x_ref[...], w_ref[...])

    @pl.when(k == pl.num_programs(1) - 1)
    def _():
        r = DEEPNORM_ALPHA * h_ref[...] + acc_ref[...]
        mu = jnp.mean(r, axis=-1, keepdims=True)
        d = r - mu
        var = jnp.mean(d * d, axis=-1, keepdims=True)
        out = d * lax.rsqrt(var + LN_EPS) * g_ref[...] + b_ref[...]
        o_ref[...] = out
        ob_ref[...] = out.astype(BF16)


def _outproj_ln(x, w, h, g, b, name, tm=512, tk=512):
    T, K = x.shape
    N = w.shape[1]
    tm = min(tm, T)
    tk = min(tk, K)
    return pl.pallas_call(
        _outproj_ln_kernel,
        grid=(T // tm, K // tk),
        in_specs=[pl.BlockSpec((tm, tk), lambda i, k: (i, k)),
                  pl.BlockSpec((tk, N), lambda i, k: (k, 0)),
                  pl.BlockSpec((tm, N), lambda i, k: (i, 0)),
                  pl.BlockSpec((1, N), lambda i, k: (0, 0)),
                  pl.BlockSpec((1, N), lambda i, k: (0, 0))],
        out_specs=[pl.BlockSpec((tm, N), lambda i, k: (i, 0)),
                   pl.BlockSpec((tm, N), lambda i, k: (i, 0))],
        out_shape=[jax.ShapeDtypeStruct((T, N), F32), jax.ShapeDtypeStruct((T, N), BF16)],
        scratch_shapes=[pltpu.VMEM((tm, N), F32)],
        compiler_params=_cparams(("parallel", "arbitrary")),
        name=name,
    )(x, w, h, g.reshape(1, N), b.reshape(1, N))


def _conv_kernel(x_ref, p_ref, w_ref, b_ref, o_ref, *, taps):
    i = pl.program_id(1)
    cur = x_ref[0]
    ts = cur.shape[0]
    prev = jnp.where(i > 0, p_ref[0], 0.0)
    xc = jnp.concatenate([prev, cur], axis=0)
    acc = jnp.broadcast_to(b_ref[...], cur.shape)
    for k in range(taps):
        off = SUBLANES - (taps - 1) + k
        acc = acc + w_ref[k:k + 1, :] * xc[off:off + ts]
    o_ref[0] = _silu(acc)


def _conv_silu(src, c0, w, b, name, ts=512, tc=512):
    B, S, _ = src.shape
    taps, C = w.shape
    ts = min(ts, S)
    cb0 = c0 // tc
    assert c0 % tc == 0 and C % tc == 0
    rpb = ts // SUBLANES
    return pl.pallas_call(
        functools.partial(_conv_kernel, taps=taps),
        grid=(B, S // ts, C // tc),
        in_specs=[pl.BlockSpec((1, ts, tc), lambda bb, i, j: (bb, i, cb0 + j)),
                  pl.BlockSpec((1, SUBLANES, tc),
                               lambda bb, i, j: (bb, jnp.maximum(i * rpb - 1, 0), cb0 + j)),
                  pl.BlockSpec((taps, tc), lambda bb, i, j: (0, j)),
                  pl.BlockSpec((1, tc), lambda bb, i, j: (0, j))],
        out_specs=pl.BlockSpec((1, ts, tc), lambda bb, i, j: (bb, i, j)),
        out_shape=jax.ShapeDtypeStruct((B, S, C), F32),
        compiler_params=_cparams(("parallel", "parallel", "parallel")),
        name=name,
    )(src, src, w, b.reshape(1, C))


def _ssd_kernel(x_ref, b_ref, c_ref, z_ref, dt_ref, hp_ref, d_ref, nw_ref, o_ref,
                state_ref, y_ref):
    L, N, P = SSD_CHUNK, SSD_D_STATE, SSD_HEAD_DIM
    G, GC, HPG = SSD_N_GROUPS, SSD_GROUP_CH, SSD_HEADS_PER_GROUP
    ci = pl.program_id(1)

    @pl.when(ci == 0)
    def _():
        state_ref[...] = jnp.zeros_like(state_ref)

    lane = lax.broadcasted_iota(jnp.int32, (1, LANES), 1)
    rows = lax.broadcasted_iota(jnp.int32, (L, L), 0)
    cols = lax.broadcasted_iota(jnp.int32, (L, L), 1)
    causal = rows >= cols
    tri = causal.astype(BF16)
    lo_half = lane < P

    dt = _softplus(dt_ref[0] + hp_ref[0:1, :])
    neg_a = jnp.where(lane < SSD_N_HEADS, -jnp.exp(hp_ref[1:2, :]), 0.0)
    a_cs = _dot_sel(tri, dt * neg_a)
    a_cs_t = a_cs.T
    dt_t = dt.T
    a_last = a_cs[L - 1:L, :]
    state_w = dt * jnp.exp(a_last - a_cs)
    chunk_decay = jnp.exp(a_last)

    def pair_cols(v, h0):
        return jnp.where(lo_half, jnp.broadcast_to(v[:, h0:h0 + 1], (v.shape[0], LANES)),
                         jnp.broadcast_to(v[:, h0 + 1:h0 + 2], (v.shape[0], LANES)))

    cb, y_off, bgt = [], [], []
    for g in range(G):
        bg = b_ref[0, :, g * N:(g + 1) * N]
        cgb = c_ref[0, :, g * N:(g + 1) * N].astype(BF16)
        cb.append(_dot_nt(cgb, bg.astype(BF16)))
        y_off.append(_dot(cgb, state_ref[g].astype(BF16)))
        bgt.append(bg.T.astype(BF16))

    for g in range(G):
        xs = []
        for j in range(HPG // 2):
            h0 = g * HPG + 2 * j
            sl = slice(g * GC + j * LANES, g * GC + (j + 1) * LANES)
            xp = x_ref[0, :, sl]
            xb = xp.astype(BF16)
            zero = jnp.zeros_like(xb)
            m = []
            for h in (h0, h0 + 1):
                col = jnp.broadcast_to(a_cs[:, h:h + 1], (L, L))
                decay = jnp.exp(jnp.where(causal, col - a_cs_t[h:h + 1, :], -jnp.inf))
                m.append((cb[g] * decay * dt_t[h:h + 1, :]).astype(BF16))
            y = _dot(jnp.concatenate(m, axis=1),
                     jnp.concatenate([jnp.where(lo_half, xb, zero), jnp.where(lo_half, zero, xb)], axis=0))
            y = y + y_off[g][:, j * LANES:(j + 1) * LANES] * jnp.exp(pair_cols(a_cs, h0))
            y = y + xp * d_ref[:, sl]
            y_ref[:, sl] = y * _silu(z_ref[0, :, sl])
            xs.append((xp * pair_cols(state_w, h0)).astype(BF16))
        cd = jnp.concatenate([pair_cols(chunk_decay, g * HPG + 2 * j) for j in range(HPG // 2)], axis=1)
        state_ref[g] = state_ref[g] * cd + _dot(bgt[g], jnp.concatenate(xs, axis=1))

    for g in range(G):
        gsl = slice(g * GC, (g + 1) * GC)
        yg = y_ref[:, gsl]
        ms = jnp.sum(yg * yg, axis=-1, keepdims=True) * (1.0 / GC)
        o_ref[0, :, gsl] = (yg * lax.rsqrt(ms + RMS_EPS) * nw_ref[:, gsl]).astype(o_ref.dtype)


def _ssd_scan(xbc, proj, dtp, hp, dskip, normw, name):
    B, S, _ = xbc.shape
    L, G, N, DI = SSD_CHUNK, SSD_N_GROUPS, SSD_D_STATE, SSD_D_INNER
    nc = S // L
    b_blk0 = DI // SSD_BC
    c_blk0 = (DI + SSD_BC) // SSD_BC
    return pl.pallas_call(
        _ssd_kernel,
        grid=(B, nc),
        in_specs=[pl.BlockSpec((1, L, DI), lambda b, c: (b, c, 0)),
                  pl.BlockSpec((1, L, SSD_BC), lambda b, c: (b, c, b_blk0)),
                  pl.BlockSpec((1, L, SSD_BC), lambda b, c: (b, c, c_blk0)),
                  pl.BlockSpec((1, L, DI), lambda b, c: (b, c, 0)),
                  pl.BlockSpec((1, L, LANES), lambda b, c: (b, c, 0)),
                  pl.BlockSpec((SUBLANES, LANES), lambda b, c: (0, 0)),
                  pl.BlockSpec((1, DI), lambda b, c: (0, 0)),
                  pl.BlockSpec((1, DI), lambda b, c: (0, 0))],
        out_specs=pl.BlockSpec((1, L, DI), lambda b, c: (b, c, 0)),
        out_shape=jax.ShapeDtypeStruct((B, S, DI), BF16),
        scratch_shapes=[pltpu.VMEM((G, N, SSD_GROUP_CH), F32), pltpu.VMEM((L, DI), F32)],
        compiler_params=_cparams(("parallel", "arbitrary")),
        name=name,
    )(xbc, xbc, xbc, proj, dtp, hp, dskip, normw)


def _pad_heads(a, groups):
    per = a.shape[-1] // groups
    a = a.reshape(a.shape[:-1] + (groups, per))
    a = jnp.pad(a, [(0, 0)] * (a.ndim - 1) + [(0, LANES - per)])
    return a.reshape(a.shape[:-2] + (groups * LANES,))


def _ssd_layer(h, hb, in_w, conv_w, conv_b, dt_bias, a_log, d_skip, norm_w, out_w, ln_g, ln_b, tag):
    B, S, D = h.shape
    T = B * S
    n_zx = SSD_D_INNER + SSD_CONV_DIM
    w_zx = in_w[:, :n_zx].astype(BF16)
    w_dt = jnp.pad(in_w[:, n_zx:], ((0, 0), (0, LANES - SSD_N_HEADS))).astype(BF16)
    proj = _matmul(hb, w_zx, f"{tag}_in").reshape(B, S, n_zx)
    dtp = _matmul(hb, w_dt, f"{tag}_dt").reshape(B, S, LANES)
    xbc = _conv_silu(proj, SSD_D_INNER, conv_w, conv_b, f"{tag}_conv")
    hp = jnp.zeros((SUBLANES, LANES), F32)
    hp = hp.at[0, :SSD_N_HEADS].set(dt_bias)
    hp = hp.at[1, :SSD_N_HEADS].set(a_log)
    dskip = jnp.repeat(d_skip, SSD_HEAD_DIM).reshape(1, SSD_D_INNER)
    y = _ssd_scan(xbc, proj, dtp, hp, dskip, norm_w.reshape(1, SSD_D_INNER), f"{tag}_scan")
    hn, hnb = _outproj_ln(y.reshape(T, SSD_D_INNER), out_w.astype(BF16), h.reshape(T, D),
                          ln_g, ln_b, f"{tag}_out")
    return hn.reshape(B, S, D), hnb


def _rope_tables(pos_ref, invf_ref, cc_ref, ss_ref):
    lane = lax.broadcasted_iota(jnp.int32, (1, LANES), 1)
    ang = pos_ref[0].astype(F32) * invf_ref[...]
    cc_ref[...] = jnp.cos(ang)
    ss_ref[...] = jnp.where(lane < MLA_ROPE // 2, -1.0, 1.0) * jnp.sin(ang)


def _mla_q_kernel(qc_ref, nw_ref, w_ref, pos_ref, invf_ref, o_ref, xn_ref, cc_ref, ss_ref):
    hd = pl.program_id(2)

    @pl.when(hd == 0)
    def _():
        x = qc_ref[0]
        ms = jnp.mean(x * x, axis=-1, keepdims=True)
        xn_ref[...] = (x * lax.rsqrt(ms + RMS_EPS) * nw_ref[...]).astype(BF16)
        _rope_tables(pos_ref, invf_ref, cc_ref, ss_ref)

    r = _dot(xn_ref[...], w_ref[0])
    scale = MLA_QK ** -0.5
    roped = r[:, LANES:2 * LANES] * cc_ref[...] + r[:, 2 * LANES:] * ss_ref[...]
    o_ref[0, 0, :, :MLA_NOPE] = (r[:, :MLA_NOPE] * scale).astype(BF16)
    o_ref[0, 0, :, MLA_NOPE:] = (roped[:, :MLA_ROPE] * scale).astype(BF16)


def _mla_kv_kernel(kvc_ref, nw_ref, w_ref, kr_ref, pos_ref, invf_ref, k_ref, v_ref,
                   xn_ref, kr_s):
    hd = pl.program_id(2)

    @pl.when(hd == 0)
    def _():
        x = kvc_ref[0]
        ms = jnp.mean(x * x, axis=-1, keepdims=True)
        xn_ref[...] = (x * lax.rsqrt(ms + RMS_EPS) * nw_ref[...]).astype(BF16)
        lane = lax.broadcasted_iota(jnp.int32, (1, LANES), 1)
        ang = pos_ref[0].astype(F32) * invf_ref[...]
        cc = jnp.cos(ang)
        ss = jnp.where(lane < MLA_ROPE // 2, -1.0, 1.0) * jnp.sin(ang)
        kr_s[...] = kr_ref[0, :, :LANES] * cc + kr_ref[0, :, LANES:] * ss

    r = _dot(xn_ref[...], w_ref[0])
    k_ref[0, 0, :, :MLA_NOPE] = r[:, :MLA_NOPE].astype(BF16)
    k_ref[0, 0, :, MLA_NOPE:] = kr_s[:, :MLA_ROPE].astype(BF16)
    v_ref[0, 0] = r[:, MLA_NOPE:].astype(BF16)


def _mla_attn_kernel(q_ref, k_ref, v_ref, z_ref, o_ref, *, tq):
    qi = pl.program_id(2)
    q = q_ref[0, 0]

    def block(kb, carry, masked):
        m, l, acc = carry
        start = pl.multiple_of(kb * tq, tq)
        k = k_ref[0, 0, pl.ds(start, tq), :]
        v = v_ref[0, 0, pl.ds(start, tq), :]
        s = _dot_nt(q, k)
        if masked:
            rows = lax.broadcasted_iota(jnp.int32, (tq, tq), 0)
            cols = lax.broadcasted_iota(jnp.int32, (tq, tq), 1)
            s = jnp.where(rows >= cols, s, -jnp.inf)
        m_new = jnp.maximum(m, jnp.max(s, axis=-1, keepdims=True))
        p = jnp.exp(s - m_new)
        alpha = jnp.exp(m - m_new)
        l = alpha * l + jnp.sum(p, axis=-1, keepdims=True)
        acc = alpha * acc + _dot(p.astype(BF16), v)
        return m_new, l, acc

    init = (jnp.full((tq, 1), -jnp.inf, F32), jnp.zeros((tq, 1), F32), jnp.zeros((tq, MLA_V), F32))
    carry = lax.fori_loop(0, qi, lambda kb, c: block(kb, c, False), init)
    m, l, acc = block(qi, carry, True)
    o = acc / l
    o_ref[0] = (o * _silu(z_ref[0])).astype(o_ref.dtype)


def _mla_layer(h, hb, positions, in_w, q_norm_w, q_up_w, kv_norm_w, kv_up_w, out_w, ln_g, ln_b, tag):
    B, S, D = h.shape
    T = B * S
    H = MLA_N_HEADS
    o_kv = MLA_Q_RANK
    o_kr = o_kv + MLA_KV_RANK
    o_z = o_kr + MLA_ROPE
    half = MLA_ROPE // 2
    w_kr = in_w[:, o_kr:o_z]
    w_kr_sw = jnp.concatenate([w_kr[:, half:], w_kr[:, :half]], axis=1)
    zpad = jnp.zeros((D, LANES - MLA_ROPE), F32)
    w_kr2 = jnp.concatenate([w_kr, zpad, w_kr_sw, zpad], axis=1)
    q_c = _matmul(hb, in_w[:, :o_kv].astype(BF16), f"{tag}_qc", tn=768).reshape(B, S, MLA_Q_RANK)
    kv_c = _matmul(hb, in_w[:, o_kv:o_kr].astype(BF16), f"{tag}_kvc").reshape(B, S, MLA_KV_RANK)
    kr = _matmul(hb, w_kr2.astype(BF16), f"{tag}_kr").reshape(B, S, 2 * LANES)
    z = _matmul(hb, in_w[:, o_z:].astype(BF16), f"{tag}_z").reshape(B, S, MLA_GATE)

    wq = q_up_w.reshape(MLA_Q_RANK, H, MLA_QK)
    wq_n, wq_r = wq[:, :, :MLA_NOPE], wq[:, :, MLA_NOPE:]
    wq_sw = jnp.concatenate([wq_r[:, :, half:], wq_r[:, :, :half]], axis=2)
    zq = jnp.zeros((MLA_Q_RANK, H, LANES - MLA_ROPE), F32)
    wq3 = jnp.concatenate([wq_n, wq_r, zq, wq_sw, zq], axis=2)
    wq3 = jnp.transpose(wq3, (1, 0, 2)).astype(BF16)
    wkv = jnp.transpose(kv_up_w.reshape(MLA_KV_RANK, H, MLA_NOPE + MLA_V), (1, 0, 2)).astype(BF16)

    inv_freq = ROPE_THETA ** (-jnp.arange(0, MLA_ROPE, 2, dtype=F32) / MLA_ROPE)
    invf = jnp.concatenate([inv_freq, inv_freq, jnp.zeros((LANES - MLA_ROPE,), F32)]).reshape(1, LANES)
    pos3 = positions.reshape(B, S, 1)

    tm = min(512, S)
    qf = pl.pallas_call(
        _mla_q_kernel,
        grid=(B, S // tm, H),
        in_specs=[pl.BlockSpec((1, tm, MLA_Q_RANK), lambda b, i, hd: (b, i, 0)),
                  pl.BlockSpec((1, MLA_Q_RANK), lambda b, i, hd: (0, 0)),
                  pl.BlockSpec((1, MLA_Q_RANK, 3 * LANES), lambda b, i, hd: (hd, 0, 0)),
                  pl.BlockSpec((1, tm, 1), lambda b, i, hd: (b, i, 0)),
                  pl.BlockSpec((1, LANES), lambda b, i, hd: (0, 0))],
        out_specs=pl.BlockSpec((1, 1, tm, MLA_QK), lambda b, i, hd: (b, hd, i, 0)),
        out_shape=jax.ShapeDtypeStruct((B, H, S, MLA_QK), BF16),
        scratch_shapes=[pltpu.VMEM((tm, MLA_Q_RANK), BF16), pltpu.VMEM((tm, LANES), F32),
                        pltpu.VMEM((tm, LANES), F32)],
        compiler_params=_cparams(("parallel", "parallel", "arbitrary")),
        name=f"{tag}_qup",
    )(q_c, q_norm_w.reshape(1, -1), wq3, pos3, invf)

    kf, vf = pl.pallas_call(
        _mla_kv_kernel,
        grid=(B, S // tm, H),
        in_specs=[pl.BlockSpec((1, tm, MLA_KV_RANK), lambda b, i, hd: (b, i, 0)),
                  pl.BlockSpec((1, MLA_KV_RANK), lambda b, i, hd: (0, 0)),
                  pl.BlockSpec((1, MLA_KV_RANK, MLA_NOPE + MLA_V), lambda b, i, hd: (hd, 0, 0)),
                  pl.BlockSpec((1, tm, 2 * LANES), lambda b, i, hd: (b, i, 0)),
                  pl.BlockSpec((1, tm, 1), lambda b, i, hd: (b, i, 0)),
                  pl.BlockSpec((1, LANES), lambda b, i, hd: (0, 0))],
        out_specs=[pl.BlockSpec((1, 1, tm, MLA_QK), lambda b, i, hd: (b, hd, i, 0)),
                   pl.BlockSpec((1, 1, tm, MLA_V), lambda b, i, hd: (b, hd, i, 0))],
        out_shape=[jax.ShapeDtypeStruct((B, H, S, MLA_QK), BF16),
                   jax.ShapeDtypeStruct((B, H, S, MLA_V), BF16)],
        scratch_shapes=[pltpu.VMEM((tm, MLA_KV_RANK), BF16), pltpu.VMEM((tm, LANES), F32)],
        compiler_params=_cparams(("parallel", "parallel", "arbitrary")),
        name=f"{tag}_kvup",
    )(kv_c, kv_norm_w.reshape(1, -1), wkv, kr, pos3, invf)

    tq = min(512, S)
    o = pl.pallas_call(
        functools.partial(_mla_attn_kernel, tq=tq),
        grid=(B, H, S // tq),
        in_specs=[pl.BlockSpec((1, 1, tq, MLA_QK), lambda b, hd, i: (b, hd, i, 0)),
                  pl.BlockSpec((1, 1, S, MLA_QK), lambda b, hd, i: (b, hd, 0, 0)),
                  pl.BlockSpec((1, 1, S, MLA_V), lambda b, hd, i: (b, hd, 0, 0)),
                  pl.BlockSpec((1, tq, MLA_V), lambda b, hd, i: (b, i, hd))],
        out_specs=pl.BlockSpec((1, tq, MLA_V), lambda b, hd, i: (b, i, hd)),
        out_shape=jax.ShapeDtypeStruct((B, S, MLA_GATE), BF16),
        compiler_params=_cparams(("parallel", "parallel", "arbitrary")),
        name=f"{tag}_attn",
    )(qf, kf, vf, z)

    hn, hnb = _outproj_ln(o.reshape(T, MLA_GATE), out_w.astype(BF16), h.reshape(T, D),
                          ln_g, ln_b, f"{tag}_out")
    return hn.reshape(B, S, D), hnb


GDN_QK_PER_STEP = 4
GDN_V_PER_STEP = 2 * GDN_QK_PER_STEP
GDN_CHUNKS_PER_STEP = 4
GDN_ROWS = GDN_CHUNKS_PER_STEP * GDN_CHUNK


def _block_mask(nrows, ncols, rblk, cblk, cmod=None):
    r = lax.broadcasted_iota(jnp.int32, (nrows, ncols), 0) // rblk
    c = lax.broadcasted_iota(jnp.int32, (nrows, ncols), 1) // cblk
    if cmod is not None:
        c = c % cmod
    return jnp.where(r == c, 1.0, 0.0).astype(BF16)


def _block_diag(x, reps, mask):
    return jnp.concatenate([x] * reps, axis=0) * mask


def _gdn_kernel(q_ref, k_ref, v_ref, z_ref, b_ref, a_ref, hp_ref, nw_ref, o_ref,
                state_ref, u_s, wq_s, kd_s, qkm_s, egl_s, qn_s, kn_s):
    L, R, C = GDN_CHUNK, GDN_ROWS, GDN_CHUNKS_PER_STEP
    NQ, NV = GDN_QK_PER_STEP, GDN_V_PER_STEP
    W = NV * L
    ri = pl.program_id(2)

    @pl.when(ri == 0)
    def _():
        state_ref[...] = jnp.zeros_like(state_ref)

    beta = _sigmoid(b_ref[0])
    g = -jnp.exp(hp_ref[0, 0:1, :]) * _softplus(a_ref[0] + hp_ref[0, 1:2, :])
    rr = lax.broadcasted_iota(jnp.int32, (R, R), 0)
    cc = lax.broadcasted_iota(jnp.int32, (R, R), 1)
    blocktri = ((rr >= cc) & (rr // L == cc // L)).astype(BF16)
    gcs = _dot_sel(blocktri, g)
    gcs_t = gcs.T
    beta_t = beta.T

    lane = lax.broadcasted_iota(jnp.int32, (1, LANES), 1)
    lo = lane < L
    rows_w = lax.broadcasted_iota(jnp.int32, (L, W), 0)
    s_w = lax.broadcasted_iota(jnp.int32, (L, W), 1) % L
    incl_w = rows_w >= s_w
    strict_w = rows_w > s_w
    eye_w = jnp.where(rows_w == s_w, 1.0, 0.0)
    head_sel = (lax.broadcasted_iota(jnp.int32, (NV, W), 0)
                == lax.broadcasted_iota(jnp.int32, (NV, W), 1) // L)
    m_kk = _block_mask(NV * L, NQ * GDN_DK, 2 * L, GDN_DK)
    m_p = _block_mask(4 * L, 4 * L, L, L)
    m_uw = _block_mask(2 * L, 4 * GDN_DV, L, GDN_DV, cmod=2)
    m_s = _block_mask(2 * GDN_DK, 2 * GDN_DV, GDN_DK, GDN_DV)
    m_v = _block_mask(2 * L, 2 * GDN_DV, L, GDN_DV)
    nw = nw_ref[...]

    def row_vec(t_heads, c):
        x = t_heads[0:NV, c * L:(c + 1) * L]
        x8 = jnp.concatenate([x] * NV, axis=1)
        return jnp.sum(jnp.where(head_sel, x8, 0.0), axis=0, keepdims=True)

    def col_wide(cols):
        return jnp.concatenate([jnp.where(lo, cols[2 * j], cols[2 * j + 1]) for j in range(NV // 2)], axis=1)

    chunks = range(C)
    rows_of = [slice(c * L, (c + 1) * L) for c in chunks]
    kq = []
    for c in chunks:
        qn, kn = [], []
        for i in range(NQ):
            q = q_ref[0, rows_of[c], i * GDN_DK:(i + 1) * GDN_DK]
            k = k_ref[0, rows_of[c], i * GDN_DK:(i + 1) * GDN_DK]
            qn.append(q * lax.rsqrt(jnp.sum(q * q, axis=-1, keepdims=True) + RMS_EPS) * (GDN_DK ** -0.5))
            kn.append(k * lax.rsqrt(jnp.sum(k * k, axis=-1, keepdims=True) + RMS_EPS))
        qn_s[c] = jnp.concatenate(qn, axis=1)
        kn_s[c] = jnp.concatenate(kn, axis=1)
        qb4 = jnp.concatenate([x.astype(BF16) for x in qn], axis=1)
        kb4 = jnp.concatenate([x.astype(BF16) for x in kn], axis=1)
        kq.append(_dot_nt(jnp.concatenate([qb4, kb4], axis=0), _block_diag(kb4, NV, m_kk)))

    p_cur, t_cur = {}, {}
    for c in chunks:
        sl = rows_of[c]
        colg = [jnp.broadcast_to(gcs[sl, h:h + 1], (L, LANES)) for h in range(NV)]
        colb = [jnp.broadcast_to(beta[sl, h:h + 1], (L, LANES)) for h in range(NV)]
        decay = jnp.exp(jnp.where(incl_w, col_wide(colg) - row_vec(gcs_t, c), -jnp.inf))
        a_w = jnp.where(strict_w, kq[c][L:] * col_wide(colb) * decay, 0.0)
        qkm_s[c] = (kq[c][:L] * decay).astype(BF16)
        for gi in range(NV // 4):
            gs = slice(gi * 4 * L, (gi + 1) * 4 * L)
            t_cur[c, gi] = eye_w[:, gs] - a_w[:, gs]
            pb = a_w[:, gs].astype(BF16)
            p_cur[c, gi] = _dot(pb, _block_diag(pb, 4, m_p))

    for it in range(5):
        for key in sorted(p_cur):
            pb = p_cur[key].astype(BF16)
            bd = _block_diag(pb, 4, m_p)
            t = t_cur[key]
            if it < 4:
                rr2 = _dot(jnp.concatenate([pb, t.astype(BF16)], axis=0), bd)
                p_cur[key] = rr2[:L]
                t_cur[key] = t + rr2[L:]
            else:
                t_cur[key] = t + _dot(t.astype(BF16), bd)

    for c in chunks:
        sl = rows_of[c]
        t_w = jnp.concatenate([t_cur[c, gi] for gi in range(NV // 4)], axis=1)
        tb_w = (t_w * row_vec(beta_t, c)).astype(BF16)
        colg = [jnp.broadcast_to(gcs[sl, h:h + 1], (L, LANES)) for h in range(NV)]
        eg = [jnp.exp(x) for x in colg]
        gl = [gcs[c * L + L - 1:c * L + L, h:h + 1] for h in range(NV)]
        for i in range(NQ):
            h0, h1 = 2 * i, 2 * i + 1
            psl = slice(i * 2 * GDN_DV, (i + 1) * 2 * GDN_DV)
            qni = qn_s[c, :, i * GDN_DK:(i + 1) * GDN_DK]
            kni = kn_s[c, :, i * GDN_DK:(i + 1) * GDN_DK]
            base = jnp.concatenate([v_ref[0, sl, psl].astype(BF16),
                                    (kni * eg[h0]).astype(BF16), (kni * eg[h1]).astype(BF16)], axis=1)
            uw = _dot(tb_w[:, i * 2 * L:(i + 1) * 2 * L], _block_diag(base, 2, m_uw))
            u_s[c, :, psl] = uw[:, :2 * GDN_DV]
            qd = jnp.concatenate([(qni * eg[h0]).astype(BF16), (qni * eg[h1]).astype(BF16)], axis=1)
            wq_s[c, i] = jnp.concatenate([uw[:, 2 * GDN_DV:].astype(BF16), qd], axis=0)
            kd_s[c, i] = jnp.concatenate([(kni * jnp.exp(gl[h0] - colg[h0])).astype(BF16),
                                          (kni * jnp.exp(gl[h1] - colg[h1])).astype(BF16)], axis=0)
            egl_s[c, i] = jnp.concatenate([jnp.broadcast_to(jnp.exp(gl[h0]), (SUBLANES, GDN_DV)),
                                           jnp.broadcast_to(jnp.exp(gl[h1]), (SUBLANES, GDN_DV))], axis=1)

    pairs = range(NQ)
    for c in chunks:
        sl = rows_of[c]
        st = [state_ref[i] for i in pairs]
        r1 = [_dot(wq_s[c, i], _block_diag(st[i].astype(BF16), 2, m_s)) for i in pairs]
        bdv = []
        for i in pairs:
            psl = slice(i * 2 * GDN_DV, (i + 1) * 2 * GDN_DV)
            vb = (u_s[c, :, psl] - r1[i][:L]).astype(BF16)
            bdv.append(_block_diag(vb, 2, m_v))
        for i in pairs:
            state_ref[i] = st[i] * egl_s[c, i, 0:1, :] + _dot_tn(kd_s[c, i], bdv[i])
        for i in pairs:
            o = r1[i][L:] + _dot(qkm_s[c, :, i * 2 * L:(i + 1) * 2 * L], bdv[i])
            for e in range(2):
                oh = o[:, e * GDN_DV:(e + 1) * GDN_DV]
                on = oh * lax.rsqrt(jnp.mean(oh * oh, axis=-1, keepdims=True) + RMS_EPS) * nw
                hsl = slice((2 * i + e) * GDN_DV, (2 * i + e + 1) * GDN_DV)
                o_ref[0, sl, hsl] = (on * _silu(z_ref[0, sl, hsl])).astype(o_ref.dtype)


def _gdn_core(qkv, proj, bp, ap, hp, normw, name):
    B, S, _ = qkv.shape
    R = min(GDN_ROWS, S)
    assert R == GDN_ROWS
    HG = GDN_N_QK_HEADS // GDN_QK_PER_STEP
    wq = GDN_QK_PER_STEP * GDN_DK
    wv = GDN_V_PER_STEP * GDN_DV
    k_blk0 = GDN_KEY_DIM // wq
    v_blk0 = 2 * GDN_KEY_DIM // wv
    z_blk0 = GDN_CONV_DIM // wv
    return pl.pallas_call(
        _gdn_kernel,
        grid=(B, HG, S // R),
        in_specs=[pl.BlockSpec((1, R, wq), lambda b, g, r: (b, r, g)),
                  pl.BlockSpec((1, R, wq), lambda b, g, r: (b, r, k_blk0 + g)),
                  pl.BlockSpec((1, R, wv), lambda b, g, r: (b, r, v_blk0 + g)),
                  pl.BlockSpec((1, R, wv), lambda b, g, r: (b, r, z_blk0 + g)),
                  pl.BlockSpec((1, R, LANES), lambda b, g, r: (b, r, g)),
                  pl.BlockSpec((1, R, LANES), lambda b, g, r: (b, r, g)),
                  pl.BlockSpec((1, SUBLANES, LANES), lambda b, g, r: (g, 0, 0)),
                  pl.BlockSpec((1, GDN_DV), lambda b, g, r: (0, 0))],
        out_specs=pl.BlockSpec((1, R, wv), lambda b, g, r: (b, r, g)),
        out_shape=jax.ShapeDtypeStruct((B, S, GDN_VAL_DIM), BF16),
        scratch_shapes=[pltpu.VMEM((GDN_QK_PER_STEP, GDN_DK, 2 * GDN_DV), F32),
                        pltpu.VMEM((GDN_CHUNKS_PER_STEP, GDN_CHUNK, wv), F32),
                        pltpu.VMEM((GDN_CHUNKS_PER_STEP, GDN_QK_PER_STEP, 2 * GDN_CHUNK, 2 * GDN_DV), BF16),
                        pltpu.VMEM((GDN_CHUNKS_PER_STEP, GDN_QK_PER_STEP, 2 * GDN_CHUNK, GDN_DK), BF16),
                        pltpu.VMEM((GDN_CHUNKS_PER_STEP, GDN_CHUNK, GDN_V_PER_STEP * GDN_CHUNK), BF16),
                        pltpu.VMEM((GDN_CHUNKS_PER_STEP, GDN_QK_PER_STEP, SUBLANES, 2 * GDN_DV), F32),
                        pltpu.VMEM((GDN_CHUNKS_PER_STEP, GDN_CHUNK, wq), F32),
                        pltpu.VMEM((GDN_CHUNKS_PER_STEP, GDN_CHUNK, wq), F32)],
        compiler_params=_cparams(("parallel", "parallel", "arbitrary")),
        name=name,
    )(qkv, qkv, qkv, proj, bp, ap, hp, normw)


def _gdn_layer(h, hb, in_w, conv_w, a_log, dt_bias, norm_w, out_w, ln_g, ln_b, tag):
    B, S, D = h.shape
    T = B * S
    n_qz = GDN_CONV_DIM + GDN_VAL_DIM
    HG = GDN_N_QK_HEADS // GDN_QK_PER_STEP
    proj = _matmul(hb, in_w[:, :n_qz].astype(BF16), f"{tag}_in").reshape(B, S, n_qz)
    ba = _matmul(hb, in_w[:, n_qz:].astype(BF16), f"{tag}_ba").reshape(B, S, 2 * GDN_N_V_HEADS)
    qkv = _conv_silu(proj, 0, conv_w, jnp.zeros((GDN_CONV_DIM,), F32), f"{tag}_conv")
    bp = _pad_heads(ba[..., :GDN_N_V_HEADS], HG)
    ap = _pad_heads(ba[..., GDN_N_V_HEADS:], HG)
    hp = jnp.zeros((HG, SUBLANES, LANES), F32)
    hp = hp.at[:, 0, :GDN_V_PER_STEP].set(a_log.reshape(HG, -1))
    hp = hp.at[:, 1, :GDN_V_PER_STEP].set(dt_bias.reshape(HG, -1))
    o = _gdn_core(qkv, proj, bp, ap, hp, norm_w.reshape(1, GDN_DV), f"{tag}_core")
    hn, hnb = _outproj_ln(o.reshape(T, GDN_VAL_DIM), out_w.astype(BF16), h.reshape(T, D),
                          ln_g, ln_b, f"{tag}_out")
    return hn.reshape(B, S, D), hnb


def kernel(x, positions, ssd_in_w, ssd_conv_w, ssd_conv_b, ssd_dt_bias, ssd_a_log, ssd_d, ssd_norm_w,
           ssd_out_w, mla_in_w, mla_q_norm_w, mla_q_up_w, mla_kv_norm_w, mla_kv_up_w, mla_out_w,
           gdn_in_w, gdn_conv_w, gdn_a_log, gdn_dt_bias, gdn_norm_w, gdn_out_w, ln_g, ln_b):
    B, S, D = x.shape
    h = x
    hb = x.reshape(B * S, D).astype(BF16)
    for i in range(DEPTH):
        kind, j = i % 3, i // 3
        if kind == 0:
            h, hb = _ssd_layer(h, hb, ssd_in_w[j], ssd_conv_w[j], ssd_conv_b[j], ssd_dt_bias[j],
                               ssd_a_log[j], ssd_d[j], ssd_norm_w[j], ssd_out_w[j], ln_g[i], ln_b[i],
                               f"l{i}_ssd")
        elif kind == 1:
            h, hb = _mla_layer(h, hb, positions, mla_in_w[j], mla_q_norm_w[j], mla_q_up_w[j],
                               mla_kv_norm_w[j], mla_kv_up_w[j], mla_out_w[j], ln_g[i], ln_b[i],
                               f"l{i}_mla")
        else:
            h, hb = _gdn_layer(h, hb, gdn_in_w[j], gdn_conv_w[j], gdn_a_log[j], gdn_dt_bias[j],
                               gdn_norm_w[j], gdn_out_w[j], ln_g[i], ln_b[i], f"l{i}_gdn")
    return h
```

```python
import functools
import math

import jax
import jax.numpy as jnp
from jax import lax
from jax.experimental import pallas as pl
from jax.experimental.pallas import tpu as pltpu

F32 = jnp.float32
BF16 = jnp.bfloat16

D_MODEL = 2048
DEPTH = 4
DEEPNORM_ALPHA = (2.0 * DEPTH) ** 0.25
LN_EPS = 1e-5
RMS_EPS = 1e-6

SSD_D_INNER = 4096
SSD_HEAD_DIM = 64
SSD_N_HEADS = 64
SSD_N_GROUPS = 8
SSD_HEADS_PER_GROUP = SSD_N_HEADS // SSD_N_GROUPS
SSD_D_STATE = 128
SSD_CONV = 4
SSD_CHUNK = 128
SSD_GROUP_CH = SSD_D_INNER // SSD_N_GROUPS
SSD_BC = SSD_N_GROUPS * SSD_D_STATE
SSD_CONV_DIM = SSD_D_INNER + 2 * SSD_BC

MLA_N_HEADS = 16
MLA_Q_RANK = 768
MLA_KV_RANK = 512
MLA_NOPE = 128
MLA_ROPE = 64
MLA_V = 128
MLA_QK = MLA_NOPE + MLA_ROPE
MLA_GATE = MLA_N_HEADS * MLA_V
ROPE_THETA = 10000.0

GDN_N_QK_HEADS = 16
GDN_N_V_HEADS = 32
GDN_DK = 128
GDN_DV = 128
GDN_KEY_DIM = GDN_N_QK_HEADS * GDN_DK
GDN_VAL_DIM = GDN_N_V_HEADS * GDN_DV
GDN_CONV = 4
GDN_CHUNK = 64
GDN_CONV_DIM = 2 * GDN_KEY_DIM + GDN_VAL_DIM

LANES = 128
SUBLANES = 8
VMEM_LIMIT_BYTES = 56 * 1024 * 1024


def _cparams(sem):
    return pltpu.CompilerParams(dimension_semantics=sem, vmem_limit_bytes=VMEM_LIMIT_BYTES)


def _sigmoid(x):
    return 1.0 / (1.0 + jnp.exp(-x))


def _silu(x):
    return x * _sigmoid(x)


def _softplus(x):
    return jnp.maximum(x, 0.0) + jnp.log1p(jnp.exp(-jnp.abs(x)))


def _dot(a, b):
    return jnp.dot(a, b, preferred_element_type=F32)


def _dot_nt(a, b):
    return lax.dot_general(a, b, (((1,), (1,)), ((), ())), preferred_element_type=F32)


def _dot_tn(a, b):
    return lax.dot_general(a, b, (((0,), (0,)), ((), ())), preferred_element_type=F32)


def _dot_sel(sel_bf16, a):
    hi = a.astype(BF16)
    r1 = a - hi.astype(F32)
    mid = r1.astype(BF16)
    lo = (r1 - mid.astype(F32)).astype(BF16)
    return _dot(sel_bf16, hi) + _dot(sel_bf16, mid) + _dot(sel_bf16, lo)


def _mm_kernel(x_ref, w_ref, o_ref):
    o_ref[...] = _dot(x_ref[...], w_ref[...]).astype(o_ref.dtype)


def _pick_tile(n, pref):
    t = min(n, pref)
    while n % t:
        t -= LANES
    return t


def _matmul(x, w, name, out_dtype=F32, tm=1024, tn=512):
    T, K = x.shape
    N = w.shape[1]
    tm = min(tm, T)
    tn = _pick_tile(N, tn)
    return pl.pallas_call(
        _mm_kernel,
        grid=(T // tm, N // tn),
        in_specs=[pl.BlockSpec((tm, K), lambda i, j: (i, 0)),
                  pl.BlockSpec((K, tn), lambda i, j: (0, j))],
        out_specs=pl.BlockSpec((tm, tn), lambda i, j: (i, j)),
        out_shape=jax.ShapeDtypeStruct((T, N), out_dtype),
        compiler_params=_cparams(("parallel", "parallel")),
        name=name,
    )(x, w)


def _outproj_ln_kernel(x_ref, w_ref, h_ref, g_ref, b_ref, o_ref, ob_ref):
    k = pl.program_id(1)

    @pl.when(k == 0)
    def _():
        o_ref[...] = jnp.zeros_like(o_ref)

    o_ref[...] += _dot(x_ref[...], w_ref[...])

    @pl.when(k == pl.num_programs(1) - 1)
    def _():
        r = DEEPNORM_ALPHA * h_ref[...] + o_ref[...]
        mu = jnp.mean(r, axis=-1, keepdims=True)
        d = r - mu
        var = jnp.mean(d * d, axis=-1, keepdims=True)
        out = d * lax.rsqrt(var + LN_EPS) * g_ref[...] + b_ref[...]
        o_ref[...] = out
        ob_ref[...] = out.astype(BF16)


def _outproj_ln(x, w, h, g, b, name, tm=1024, tk=512):
    T, K = x.shape
    N = w.shape[1]
    tm = min(tm, T)
    tk = min(tk, K)
    return pl.pallas_call(
        _outproj_ln_kernel,
        grid=(T // tm, K // tk),
        in_specs=[pl.BlockSpec((tm, tk), lambda i, k: (i, k)),
                  pl.BlockSpec((tk, N), lambda i, k: (k, 0)),
                  pl.BlockSpec((tm, N), lambda i, k: (i, 0)),
                  pl.BlockSpec((1, N), lambda i, k: (0, 0)),
                  pl.BlockSpec((1, N), lambda i, k: (0, 0))],
        out_specs=[pl.BlockSpec((tm, N), lambda i, k: (i, 0)),
                   pl.BlockSpec((tm, N), lambda i, k: (i, 0))],
        out_shape=[jax.ShapeDtypeStruct((T, N), F32), jax.ShapeDtypeStruct((T, N), BF16)],
        compiler_params=_cparams(("parallel", "arbitrary")),
        name=name,
    )(x, w, h, g.reshape(1, N), b.reshape(1, N))


MXU_COLS = 256


def _column_pieces(x_ref, w_ref):
    x = x_ref[...]
    return [(c, _dot(x, w_ref[:, c:c + MXU_COLS])) for c in range(0, w_ref.shape[1], MXU_COLS)]


def _proj_silu_kernel(x_ref, w_ref, o_ref):
    for c, y in _column_pieces(x_ref, w_ref):
        o_ref[:, c:c + MXU_COLS] = _silu(y).astype(o_ref.dtype)


def _proj_conv_kernel(x_ref, w_ref, cw_ref, cb_ref, o_ref, halo_ref, *, taps, tiles_per_seq):
    i, j = pl.program_id(0), pl.program_id(1)
    tm = x_ref.shape[0]

    @pl.when(i % tiles_per_seq == 0)
    def _():
        halo_ref[j] = jnp.zeros(halo_ref.shape[1:], F32)

    for c, y in _column_pieces(x_ref, w_ref):
        cs = slice(c, c + MXU_COLS)
        acc = cb_ref[:, cs] + cw_ref[taps - 1:taps, cs] * y
        for d in range(1, taps):
            acc = acc + cw_ref[taps - 1 - d:taps - d, cs] * pltpu.roll(y, d, 0)
        o_ref[:, cs] = _silu(acc).astype(o_ref.dtype)
        head = jnp.concatenate([halo_ref[j, :, cs], y[:SUBLANES, :]], axis=0)
        halo_ref[j, :, cs] = y[tm - SUBLANES:, :]
        acc = jnp.broadcast_to(cb_ref[:, cs], (SUBLANES, MXU_COLS))
        for k in range(taps):
            off = SUBLANES - (taps - 1) + k
            acc = acc + cw_ref[k:k + 1, cs] * head[off:off + SUBLANES]
        o_ref[:SUBLANES, cs] = _silu(acc).astype(o_ref.dtype)


def _proj_silu(x, w, name, out_dtype=F32, tm=1024, tn=1024):
    T, K = x.shape
    N = w.shape[1]
    tm = min(tm, T)
    tn = _pick_tile(N, tn)
    return pl.pallas_call(
        _proj_silu_kernel,
        grid=(T // tm, N // tn),
        in_specs=[pl.BlockSpec((tm, K), lambda i, j: (i, 0)),
                  pl.BlockSpec((K, tn), lambda i, j: (0, j))],
        out_specs=pl.BlockSpec((tm, tn), lambda i, j: (i, j)),
        out_shape=jax.ShapeDtypeStruct((T, N), out_dtype),
        compiler_params=_cparams(("parallel", "parallel")),
        name=name,
    )(x, w)


def _proj_conv(x, w, conv_w, conv_b, seq_len, name, out_dtype=F32, tm=1024, tn=1024):
    T, K = x.shape
    N = w.shape[1]
    taps = conv_w.shape[0]
    tm = min(tm, seq_len)
    assert seq_len % tm == 0
    tn = _pick_tile(N, tn)
    return pl.pallas_call(
        functools.partial(_proj_conv_kernel, taps=taps, tiles_per_seq=seq_len // tm),
        grid=(T // tm, N // tn),
        in_specs=[pl.BlockSpec((tm, K), lambda i, j: (i, 0)),
                  pl.BlockSpec((K, tn), lambda i, j: (0, j)),
                  pl.BlockSpec((taps, tn), lambda i, j: (0, j)),
                  pl.BlockSpec((1, tn), lambda i, j: (0, j))],
        out_specs=pl.BlockSpec((tm, tn), lambda i, j: (i, j)),
        out_shape=jax.ShapeDtypeStruct((T, N), out_dtype),
        scratch_shapes=[pltpu.VMEM((N // tn, SUBLANES, tn), F32)],
        compiler_params=_cparams(("arbitrary", "arbitrary")),
        name=name,
    )(x, w, conv_w, conv_b.reshape(1, N))


def _ssd_kernel(x_ref, b_ref, c_ref, z_ref, dt_ref, hp_ref, d_ref, nw_ref, o_ref,
                state_ref, y_ref):
    L, N, P = SSD_CHUNK, SSD_D_STATE, SSD_HEAD_DIM
    G, GC, HPG = SSD_N_GROUPS, SSD_GROUP_CH, SSD_HEADS_PER_GROUP
    ci = pl.program_id(1)

    @pl.when(ci == 0)
    def _():
        state_ref[...] = jnp.zeros_like(state_ref)

    lane = lax.broadcasted_iota(jnp.int32, (1, LANES), 1)
    rows = lax.broadcasted_iota(jnp.int32, (L, L), 0)
    cols = lax.broadcasted_iota(jnp.int32, (L, L), 1)
    causal = rows >= cols
    tri = causal.astype(BF16)
    lo_half = lane < P

    dt = _softplus(dt_ref[0] + hp_ref[0:1, :])
    neg_a = jnp.where(lane < SSD_N_HEADS, -jnp.exp(hp_ref[1:2, :]), 0.0)
    a_cs = _dot_sel(tri, dt * neg_a)
    a_cs_t = a_cs.T
    dt_t = dt.T
    a_last = a_cs[L - 1:L, :]
    state_w = dt * jnp.exp(a_last - a_cs)
    chunk_decay = jnp.exp(a_last)

    def pair_cols(v, h0):
        return jnp.where(lo_half, jnp.broadcast_to(v[:, h0:h0 + 1], (v.shape[0], LANES)),
                         jnp.broadcast_to(v[:, h0 + 1:h0 + 2], (v.shape[0], LANES)))

    cb, y_off, bgt = [], [], []
    for g in range(G):
        bgb = b_ref[0, :, g * N:(g + 1) * N]
        cgb = c_ref[0, :, g * N:(g + 1) * N]
        cb.append(_dot_nt(cgb, bgb))
        y_off.append(_dot(cgb, state_ref[g].astype(BF16)))
        bgt.append(bgb.astype(F32).T.astype(BF16))

    for g in range(G):
        xs = []
        for j in range(HPG // 2):
            h0 = g * HPG + 2 * j
            sl = slice(g * GC + j * LANES, g * GC + (j + 1) * LANES)
            xp = x_ref[0, :, sl]
            xb = xp.astype(BF16)
            zero = jnp.zeros_like(xb)
            m = []
            for h in (h0, h0 + 1):
                col = jnp.broadcast_to(a_cs[:, h:h + 1], (L, L))
                decay = jnp.exp(jnp.where(causal, col - a_cs_t[h:h + 1, :], -jnp.inf))
                m.append((cb[g] * decay * dt_t[h:h + 1, :]).astype(BF16))
            y = _dot(jnp.concatenate(m, axis=1),
                     jnp.concatenate([jnp.where(lo_half, xb, zero), jnp.where(lo_half, zero, xb)], axis=0))
            y = y + y_off[g][:, j * LANES:(j + 1) * LANES] * jnp.exp(pair_cols(a_cs, h0))
            y = y + xp * d_ref[:, sl]
            y_ref[:, sl] = y * z_ref[0, :, sl]
            xs.append((xp * pair_cols(state_w, h0)).astype(BF16))
        cd = jnp.concatenate([pair_cols(chunk_decay, g * HPG + 2 * j) for j in range(HPG // 2)], axis=1)
        state_ref[g] = state_ref[g] * cd + _dot(bgt[g], jnp.concatenate(xs, axis=1))

    for g in range(G):
        gsl = slice(g * GC, (g + 1) * GC)
        yg = y_ref[:, gsl]
        ms = jnp.sum(yg * yg, axis=-1, keepdims=True) * (1.0 / GC)
        o_ref[0, :, gsl] = (yg * lax.rsqrt(ms + RMS_EPS) * nw_ref[:, gsl]).astype(o_ref.dtype)


def _ssd_scan(xs, bc, zs, dtp, hp, dskip, normw, name):
    B, S, _ = xs.shape
    L, G, N, DI = SSD_CHUNK, SSD_N_GROUPS, SSD_D_STATE, SSD_D_INNER
    nc = S // L
    return pl.pallas_call(
        _ssd_kernel,
        grid=(B, nc),
        in_specs=[pl.BlockSpec((1, L, DI), lambda b, c: (b, c, 0)),
                  pl.BlockSpec((1, L, SSD_BC), lambda b, c: (b, c, 0)),
                  pl.BlockSpec((1, L, SSD_BC), lambda b, c: (b, c, 1)),
                  pl.BlockSpec((1, L, DI), lambda b, c: (b, c, 0)),
                  pl.BlockSpec((1, L, LANES), lambda b, c: (b, c, 0)),
                  pl.BlockSpec((SUBLANES, LANES), lambda b, c: (0, 0)),
                  pl.BlockSpec((1, DI), lambda b, c: (0, 0)),
                  pl.BlockSpec((1, DI), lambda b, c: (0, 0))],
        out_specs=pl.BlockSpec((1, L, DI), lambda b, c: (b, c, 0)),
        out_shape=jax.ShapeDtypeStruct((B, S, DI), BF16),
        scratch_shapes=[pltpu.VMEM((G, N, SSD_GROUP_CH), F32), pltpu.VMEM((L, DI), F32)],
        compiler_params=_cparams(("parallel", "arbitrary")),
        name=name,
    )(xs, bc, bc, zs, dtp, hp, dskip, normw)


def _pad_heads(a, groups):
    per = a.shape[-1] // groups
    a = a.reshape(a.shape[:-1] + (groups, per))
    a = jnp.pad(a, [(0, 0)] * (a.ndim - 1) + [(0, LANES - per)])
    return a.reshape(a.shape[:-2] + (groups * LANES,))


def _ssd_layer(h, hb, in_w, conv_w, conv_b, dt_bias, a_log, d_skip, norm_w, out_w, ln_g, ln_b, tag):
    B, S, D = h.shape
    T = B * S
    DI = SSD_D_INNER
    o_x, o_bc, o_dt = DI, 2 * DI, DI + SSD_CONV_DIM
    w_dt = jnp.pad(in_w[:, o_dt:], ((0, 0), (0, LANES - SSD_N_HEADS))).astype(BF16)
    zs = _proj_silu(hb, in_w[:, :o_x].astype(BF16), f"{tag}_z").reshape(B, S, DI)
    xs = _proj_conv(hb, in_w[:, o_x:o_bc].astype(BF16), conv_w[:, :DI], conv_b[:DI], S,
                    f"{tag}_x").reshape(B, S, DI)
    bc = _proj_conv(hb, in_w[:, o_bc:o_dt].astype(BF16), conv_w[:, DI:], conv_b[DI:], S,
                    f"{tag}_bc", out_dtype=BF16).reshape(B, S, 2 * SSD_BC)
    dtp = _matmul(hb, w_dt, f"{tag}_dt").reshape(B, S, LANES)
    hp = jnp.zeros((SUBLANES, LANES), F32)
    hp = hp.at[0, :SSD_N_HEADS].set(dt_bias)
    hp = hp.at[1, :SSD_N_HEADS].set(a_log)
    dskip = jnp.repeat(d_skip, SSD_HEAD_DIM).reshape(1, DI)
    y = _ssd_scan(xs, bc, zs, dtp, hp, dskip, norm_w.reshape(1, DI), f"{tag}_scan")
    hn, hnb = _outproj_ln(y.reshape(T, SSD_D_INNER), out_w.astype(BF16), h.reshape(T, D),
                          ln_g, ln_b, f"{tag}_out")
    return hn.reshape(B, S, D), hnb


MLA_Q_SCALE = MLA_QK ** -0.5 * math.log2(math.e)


def _rope_table(pos_ref, invf_ref):
    lane = lax.broadcasted_iota(jnp.int32, (1, LANES), 1)
    ang = pos_ref[0].astype(F32) * invf_ref[...]
    q = MLA_ROPE // 2
    return jnp.where(lane < 2 * q, jnp.cos(ang), jnp.where(lane < 3 * q, -1.0, 1.0) * jnp.sin(ang))


def _rope_apply(xr, table):
    prod = xr * table
    return prod + pltpu.roll(prod, MLA_ROPE, 1)


def _rms_bf16(x, w):
    ms = jnp.mean(x * x, axis=-1, keepdims=True)
    return (x * lax.rsqrt(ms + RMS_EPS) * w).astype(BF16)


def _mla_q_kernel(qc_ref, nw_ref, w_ref, pos_ref, invf_ref, o_ref):
    xn = _rms_bf16(qc_ref[0], nw_ref[...])
    table = _rope_table(pos_ref, invf_ref)
    hw = MLA_NOPE + 2 * MLA_ROPE
    for hd in range(MLA_N_HEADS):
        r = _dot(xn, w_ref[:, hd * hw:(hd + 1) * hw])
        roped = _rope_apply(r[:, MLA_NOPE:], table)
        o_ref[0, hd, :, :MLA_NOPE] = (r[:, :MLA_NOPE] * MLA_Q_SCALE).astype(BF16)
        o_ref[0, hd, :, MLA_NOPE:] = (roped[:, :MLA_ROPE] * MLA_Q_SCALE).astype(BF16)


def _mla_kv_kernel(kvc_ref, nw_ref, w_ref, kr_ref, pos_ref, invf_ref, k_ref, v_ref):
    xn = _rms_bf16(kvc_ref[0], nw_ref[...])
    k_rope = _rope_apply(kr_ref[0], _rope_table(pos_ref, invf_ref))[:, :MLA_ROPE].astype(BF16)
    hw = MLA_NOPE + MLA_V
    for hd in range(MLA_N_HEADS):
        r = _dot(xn, w_ref[:, hd * hw:(hd + 1) * hw])
        k_ref[0, hd, :, :MLA_NOPE] = r[:, :MLA_NOPE].astype(BF16)
        k_ref[0, hd, :, MLA_NOPE:] = k_rope
        v_ref[0, hd] = r[:, MLA_NOPE:].astype(BF16)


def _mla_attn_kernel(q_ref, k_ref, v_ref, z_ref, o_ref, *, tk):
    qi = pl.program_id(2)
    halves = (q_ref[0, 0, :tk, :], q_ref[0, 0, tk:, :])
    rows = lax.broadcasted_iota(jnp.int32, (tk, tk), 0)
    cols = lax.broadcasted_iota(jnp.int32, (tk, tk), 1)

    def block(kb, carry, masked):
        start = pl.multiple_of(kb * tk, tk)
        k = k_ref[0, 0, pl.ds(start, tk), :]
        v = v_ref[0, 0, pl.ds(start, tk), :]
        live = [e for e in range(2) if masked[e] is not None]
        s = {e: _dot_nt(halves[e], k) for e in live}
        out = list(carry)
        for e in live:
            m, l, acc = carry[e]
            se = jnp.where(rows >= cols, s[e], -jnp.inf) if masked[e] else s[e]
            m_new = jnp.maximum(m, jnp.max(se, axis=-1, keepdims=True))
            p = jnp.exp2(se - m_new)
            alpha = jnp.exp2(m - m_new)
            l = alpha * l + jnp.sum(p, axis=-1, keepdims=True)
            out[e] = (m_new, l, alpha * acc + _dot(p.astype(BF16), v))
        return tuple(out)

    init = tuple((jnp.full((tk, 1), -jnp.inf, F32), jnp.zeros((tk, 1), F32), jnp.zeros((tk, MLA_V), F32))
                 for _ in range(2))
    carry = lax.fori_loop(0, 2 * qi, lambda kb, c: block(kb, c, (False, False)), init)
    carry = block(2 * qi, carry, (True, False))
    carry = block(2 * qi + 1, carry, (None, True))
    for e in range(2):
        _, l, acc = carry[e]
        rs = slice(e * tk, (e + 1) * tk)
        o_ref[0, rs, :] = (acc / l * z_ref[0, rs, :]).astype(o_ref.dtype)


def _mla_layer(h, hb, positions, in_w, q_norm_w, q_up_w, kv_norm_w, kv_up_w, out_w, ln_g, ln_b, tag):
    B, S, D = h.shape
    T = B * S
    H = MLA_N_HEADS
    o_kv = MLA_Q_RANK
    o_kr = o_kv + MLA_KV_RANK
    o_z = o_kr + MLA_ROPE
    half = MLA_ROPE // 2
    w_kr = in_w[:, o_kr:o_z]
    w_kr2 = jnp.concatenate([w_kr, w_kr[:, half:], w_kr[:, :half]], axis=1)
    q_c = _matmul(hb, in_w[:, :o_kv].astype(BF16), f"{tag}_qc", tn=768).reshape(B, S, MLA_Q_RANK)
    kv_c = _matmul(hb, in_w[:, o_kv:o_kr].astype(BF16), f"{tag}_kvc").reshape(B, S, MLA_KV_RANK)
    kr = _matmul(hb, w_kr2.astype(BF16), f"{tag}_kr").reshape(B, S, LANES)
    z = _proj_silu(hb, in_w[:, o_z:].astype(BF16), f"{tag}_z").reshape(B, S, MLA_GATE)

    wq = q_up_w.reshape(MLA_Q_RANK, H, MLA_QK)
    wq_r = wq[:, :, MLA_NOPE:]
    wq3 = jnp.concatenate([wq, wq_r[:, :, half:], wq_r[:, :, :half]], axis=2)
    wq3 = wq3.reshape(MLA_Q_RANK, H * (MLA_QK + MLA_ROPE)).astype(BF16)
    wkv = kv_up_w.astype(BF16)

    inv_freq = ROPE_THETA ** (-jnp.arange(0, MLA_ROPE, 2, dtype=F32) / MLA_ROPE)
    invf = jnp.tile(inv_freq, LANES // half).reshape(1, LANES)
    pos3 = positions.reshape(B, S, 1)

    tm = min(512, S)
    qf = pl.pallas_call(
        _mla_q_kernel,
        grid=(B, S // tm),
        in_specs=[pl.BlockSpec((1, tm, MLA_Q_RANK), lambda b, i: (b, i, 0)),
                  pl.BlockSpec((1, MLA_Q_RANK), lambda b, i: (0, 0)),
                  pl.BlockSpec(wq3.shape, lambda b, i: (0, 0)),
                  pl.BlockSpec((1, tm, 1), lambda b, i: (b, i, 0)),
                  pl.BlockSpec((1, LANES), lambda b, i: (0, 0))],
        out_specs=pl.BlockSpec((1, H, tm, MLA_QK), lambda b, i: (b, 0, i, 0)),
        out_shape=jax.ShapeDtypeStruct((B, H, S, MLA_QK), BF16),
        compiler_params=_cparams(("parallel", "parallel")),
        name=f"{tag}_qup",
    )(q_c, q_norm_w.reshape(1, -1), wq3, pos3, invf)

    kf, vf = pl.pallas_call(
        _mla_kv_kernel,
        grid=(B, S // tm),
        in_specs=[pl.BlockSpec((1, tm, MLA_KV_RANK), lambda b, i: (b, i, 0)),
                  pl.BlockSpec((1, MLA_KV_RANK), lambda b, i: (0, 0)),
                  pl.BlockSpec(wkv.shape, lambda b, i: (0, 0)),
                  pl.BlockSpec((1, tm, LANES), lambda b, i: (b, i, 0)),
                  pl.BlockSpec((1, tm, 1), lambda b, i: (b, i, 0)),
                  pl.BlockSpec((1, LANES), lambda b, i: (0, 0))],
        out_specs=[pl.BlockSpec((1, H, tm, MLA_QK), lambda b, i: (b, 0, i, 0)),
                   pl.BlockSpec((1, H, tm, MLA_V), lambda b, i: (b, 0, i, 0))],
        out_shape=[jax.ShapeDtypeStruct((B, H, S, MLA_QK), BF16),
                   jax.ShapeDtypeStruct((B, H, S, MLA_V), BF16)],
        compiler_params=_cparams(("parallel", "parallel")),
        name=f"{tag}_kvup",
    )(kv_c, kv_norm_w.reshape(1, -1), wkv, kr, pos3, invf)

    tq = min(1024, S)
    o = pl.pallas_call(
        functools.partial(_mla_attn_kernel, tk=tq // 2),
        grid=(B, H, S // tq),
        in_specs=[pl.BlockSpec((1, 1, tq, MLA_QK), lambda b, hd, i: (b, hd, i, 0)),
                  pl.BlockSpec((1, 1, S, MLA_QK), lambda b, hd, i: (b, hd, 0, 0)),
                  pl.BlockSpec((1, 1, S, MLA_V), lambda b, hd, i: (b, hd, 0, 0)),
                  pl.BlockSpec((1, tq, MLA_V), lambda b, hd, i: (b, i, hd))],
        out_specs=pl.BlockSpec((1, tq, MLA_V), lambda b, hd, i: (b, i, hd)),
        out_shape=jax.ShapeDtypeStruct((B, S, MLA_GATE), BF16),
        compiler_params=_cparams(("parallel", "parallel", "arbitrary")),
        name=f"{tag}_attn",
    )(qf, kf, vf, z)

    hn, hnb = _outproj_ln(o.reshape(T, MLA_GATE), out_w.astype(BF16), h.reshape(T, D),
                          ln_g, ln_b, f"{tag}_out")
    return hn.reshape(B, S, D), hnb


GDN_QK_PER_STEP = 4
GDN_V_PER_STEP = 2 * GDN_QK_PER_STEP
GDN_CHUNKS_PER_STEP = 4
GDN_ROWS = GDN_CHUNKS_PER_STEP * GDN_CHUNK


def _block_mask(nrows, ncols, rblk, cblk, cmod=None):
    r = lax.broadcasted_iota(jnp.int32, (nrows, ncols), 0) // rblk
    c = lax.broadcasted_iota(jnp.int32, (nrows, ncols), 1) // cblk
    if cmod is not None:
        c = c % cmod
    return jnp.where(r == c, 1.0, 0.0).astype(BF16)


def _block_diag(x, reps, mask):
    return jnp.concatenate([x] * reps, axis=0) * mask


def _gdn_kernel(q_ref, k_ref, v_ref, z_ref, b_ref, a_ref, hp_ref, nw_ref, o_ref,
                state_ref, u_s, wq_s, kd_s, qkm_s, egl_s, qn_s, kn_s):
    L, R, C = GDN_CHUNK, GDN_ROWS, GDN_CHUNKS_PER_STEP
    NQ, NV = GDN_QK_PER_STEP, GDN_V_PER_STEP
    W = NV * L
    ri = pl.program_id(2)

    @pl.when(ri == 0)
    def _():
        state_ref[...] = jnp.zeros_like(state_ref)

    beta = _sigmoid(b_ref[0])
    g = -jnp.exp(hp_ref[0, 0:1, :]) * _softplus(a_ref[0] + hp_ref[0, 1:2, :])
    rr = lax.broadcasted_iota(jnp.int32, (R, R), 0)
    cc = lax.broadcasted_iota(jnp.int32, (R, R), 1)
    blocktri = ((rr >= cc) & (rr // L == cc // L)).astype(BF16)
    gcs = _dot_sel(blocktri, g)
    gcs_t = gcs.T
    beta_t = beta.T

    lane = lax.broadcasted_iota(jnp.int32, (1, LANES), 1)
    lo = lane < L
    rows_w = lax.broadcasted_iota(jnp.int32, (L, W), 0)
    s_w = lax.broadcasted_iota(jnp.int32, (L, W), 1) % L
    incl_w = rows_w >= s_w
    strict_w = rows_w > s_w
    eye_w = jnp.where(rows_w == s_w, 1.0, 0.0)
    head_sel = (lax.broadcasted_iota(jnp.int32, (NV, W), 0)
                == lax.broadcasted_iota(jnp.int32, (NV, W), 1) // L)
    m_kk = _block_mask(NV * L, NQ * GDN_DK, 2 * L, GDN_DK)
    m_p = _block_mask(4 * L, 4 * L, L, L)
    m_uw = _block_mask(2 * L, 4 * GDN_DV, L, GDN_DV, cmod=2)
    m_s = _block_mask(2 * GDN_DK, 2 * GDN_DV, GDN_DK, GDN_DV)
    m_v = _block_mask(2 * L, 2 * GDN_DV, L, GDN_DV)
    nw = nw_ref[...]

    def row_vec(t_heads, c):
        x = t_heads[0:NV, c * L:(c + 1) * L]
        x8 = jnp.concatenate([x] * NV, axis=1)
        return jnp.sum(jnp.where(head_sel, x8, 0.0), axis=0, keepdims=True)

    def col_wide(cols):
        return jnp.concatenate([jnp.where(lo, cols[2 * j], cols[2 * j + 1]) for j in range(NV // 2)], axis=1)

    chunks = range(C)
    rows_of = [slice(c * L, (c + 1) * L) for c in chunks]
    kq = []
    for c in chunks:
        qn, kn = [], []
        for i in range(NQ):
            q = q_ref[0, rows_of[c], i * GDN_DK:(i + 1) * GDN_DK]
            k = k_ref[0, rows_of[c], i * GDN_DK:(i + 1) * GDN_DK]
            qn.append(q * lax.rsqrt(jnp.sum(q * q, axis=-1, keepdims=True) + RMS_EPS) * (GDN_DK ** -0.5))
            kn.append(k * lax.rsqrt(jnp.sum(k * k, axis=-1, keepdims=True) + RMS_EPS))
        qn_s[c] = jnp.concatenate(qn, axis=1)
        kn_s[c] = jnp.concatenate(kn, axis=1)
        qb4 = jnp.concatenate([x.astype(BF16) for x in qn], axis=1)
        kb4 = jnp.concatenate([x.astype(BF16) for x in kn], axis=1)
        kq.append(_dot_nt(jnp.concatenate([qb4, kb4], axis=0), _block_diag(kb4, NV, m_kk)))

    p_cur, t_cur = {}, {}
    for c in chunks:
        sl = rows_of[c]
        colg = [jnp.broadcast_to(gcs[sl, h:h + 1], (L, LANES)) for h in range(NV)]
        colb = [jnp.broadcast_to(beta[sl, h:h + 1], (L, LANES)) for h in range(NV)]
        decay = jnp.exp(jnp.where(incl_w, col_wide(colg) - row_vec(gcs_t, c), -jnp.inf))
        a_w = jnp.where(strict_w, kq[c][L:] * col_wide(colb) * decay, 0.0)
        qkm_s[c] = (kq[c][:L] * decay).astype(BF16)
        for gi in range(NV // 4):
            gs = slice(gi * 4 * L, (gi + 1) * 4 * L)
            t_cur[c, gi] = eye_w[:, gs] - a_w[:, gs]
            pb = a_w[:, gs].astype(BF16)
            p_cur[c, gi] = _dot(pb, _block_diag(pb, 4, m_p))

    for it in range(5):
        for key in sorted(p_cur):
            pb = p_cur[key].astype(BF16)
            bd = _block_diag(pb, 4, m_p)
            t = t_cur[key]
            if it < 4:
                rr2 = _dot(jnp.concatenate([pb, t.astype(BF16)], axis=0), bd)
                p_cur[key] = rr2[:L]
                t_cur[key] = t + rr2[L:]
            else:
                t_cur[key] = t + _dot(t.astype(BF16), bd)

    for c in chunks:
        sl = rows_of[c]
        t_w = jnp.concatenate([t_cur[c, gi] for gi in range(NV // 4)], axis=1)
        tb_w = (t_w * row_vec(beta_t, c)).astype(BF16)
        colg = [jnp.broadcast_to(gcs[sl, h:h + 1], (L, LANES)) for h in range(NV)]
        eg = [jnp.exp(x) for x in colg]
        gl = [gcs[c * L + L - 1:c * L + L, h:h + 1] for h in range(NV)]
        for i in range(NQ):
            h0, h1 = 2 * i, 2 * i + 1
            psl = slice(i * 2 * GDN_DV, (i + 1) * 2 * GDN_DV)
            qni = qn_s[c, :, i * GDN_DK:(i + 1) * GDN_DK]
            kni = kn_s[c, :, i * GDN_DK:(i + 1) * GDN_DK]
            base = jnp.concatenate([v_ref[0, sl, psl].astype(BF16),
                                    (kni * eg[h0]).astype(BF16), (kni * eg[h1]).astype(BF16)], axis=1)
            uw = _dot(tb_w[:, i * 2 * L:(i + 1) * 2 * L], _block_diag(base, 2, m_uw))
            u_s[c, :, psl] = uw[:, :2 * GDN_DV]
            qd = jnp.concatenate([(qni * eg[h0]).astype(BF16), (qni * eg[h1]).astype(BF16)], axis=1)
            wq_s[c, i] = jnp.concatenate([uw[:, 2 * GDN_DV:].astype(BF16), qd], axis=0)
            kd_s[c, i] = jnp.concatenate([(kni * jnp.exp(gl[h0] - colg[h0])).astype(BF16),
                                          (kni * jnp.exp(gl[h1] - colg[h1])).astype(BF16)], axis=0)
            egl_s[c, i] = jnp.concatenate([jnp.broadcast_to(jnp.exp(gl[h0]), (SUBLANES, GDN_DV)),
                                           jnp.broadcast_to(jnp.exp(gl[h1]), (SUBLANES, GDN_DV))], axis=1)

    pairs = range(NQ)
    for c in chunks:
        sl = rows_of[c]
        st = [state_ref[i] for i in pairs]
        r1 = [_dot(wq_s[c, i], _block_diag(st[i].astype(BF16), 2, m_s)) for i in pairs]
        bdv = []
        for i in pairs:
            psl = slice(i * 2 * GDN_DV, (i + 1) * 2 * GDN_DV)
            vb = (u_s[c, :, psl] - r1[i][:L]).astype(BF16)
            bdv.append(_block_diag(vb, 2, m_v))
        for i in pairs:
            state_ref[i] = st[i] * egl_s[c, i, 0:1, :] + _dot_tn(kd_s[c, i], bdv[i])
        for i in pairs:
            o = r1[i][L:] + _dot(qkm_s[c, :, i * 2 * L:(i + 1) * 2 * L], bdv[i])
            for e in range(2):
                oh = o[:, e * GDN_DV:(e + 1) * GDN_DV]
                on = oh * lax.rsqrt(jnp.mean(oh * oh, axis=-1, keepdims=True) + RMS_EPS) * nw
                hsl = slice((2 * i + e) * GDN_DV, (2 * i + e + 1) * GDN_DV)
                o_ref[0, sl, hsl] = (on * z_ref[0, sl, hsl]).astype(o_ref.dtype)


def _gdn_core(qk, v, zs, bp, ap, hp, normw, name):
    B, S, _ = qk.shape
    R = min(GDN_ROWS, S)
    assert R == GDN_ROWS
    HG = GDN_N_QK_HEADS // GDN_QK_PER_STEP
    wq = GDN_QK_PER_STEP * GDN_DK
    wv = GDN_V_PER_STEP * GDN_DV
    k_blk0 = GDN_KEY_DIM // wq
    return pl.pallas_call(
        _gdn_kernel,
        grid=(B, HG, S // R),
        in_specs=[pl.BlockSpec((1, R, wq), lambda b, g, r: (b, r, g)),
                  pl.BlockSpec((1, R, wq), lambda b, g, r: (b, r, k_blk0 + g)),
                  pl.BlockSpec((1, R, wv), lambda b, g, r: (b, r, g)),
                  pl.BlockSpec((1, R, wv), lambda b, g, r: (b, r, g)),
                  pl.BlockSpec((1, R, LANES), lambda b, g, r: (b, r, g)),
                  pl.BlockSpec((1, R, LANES), lambda b, g, r: (b, r, g)),
                  pl.BlockSpec((1, SUBLANES, LANES), lambda b, g, r: (g, 0, 0)),
                  pl.BlockSpec((1, GDN_DV), lambda b, g, r: (0, 0))],
        out_specs=pl.BlockSpec((1, R, wv), lambda b, g, r: (b, r, g)),
        out_shape=jax.ShapeDtypeStruct((B, S, GDN_VAL_DIM), BF16),
        scratch_shapes=[pltpu.VMEM((GDN_QK_PER_STEP, GDN_DK, 2 * GDN_DV), F32),
                        pltpu.VMEM((GDN_CHUNKS_PER_STEP, GDN_CHUNK, wv), F32),
                        pltpu.VMEM((GDN_CHUNKS_PER_STEP, GDN_QK_PER_STEP, 2 * GDN_CHUNK, 2 * GDN_DV), BF16),
                        pltpu.VMEM((GDN_CHUNKS_PER_STEP, GDN_QK_PER_STEP, 2 * GDN_CHUNK, GDN_DK), BF16),
                        pltpu.VMEM((GDN_CHUNKS_PER_STEP, GDN_CHUNK, GDN_V_PER_STEP * GDN_CHUNK), BF16),
                        pltpu.VMEM((GDN_CHUNKS_PER_STEP, GDN_QK_PER_STEP, SUBLANES, 2 * GDN_DV), F32),
                        pltpu.VMEM((GDN_CHUNKS_PER_STEP, GDN_CHUNK, wq), F32),
                        pltpu.VMEM((GDN_CHUNKS_PER_STEP, GDN_CHUNK, wq), F32)],
        compiler_params=_cparams(("parallel", "parallel", "arbitrary")),
        name=name,
    )(qk, qk, v, zs, bp, ap, hp, normw)


def _gdn_layer(h, hb, in_w, conv_w, a_log, dt_bias, norm_w, out_w, ln_g, ln_b, tag):
    B, S, D = h.shape
    T = B * S
    o_v, o_z, o_ba = 2 * GDN_KEY_DIM, GDN_CONV_DIM, GDN_CONV_DIM + GDN_VAL_DIM
    HG = GDN_N_QK_HEADS // GDN_QK_PER_STEP
    no_bias = jnp.zeros((GDN_CONV_DIM,), F32)
    qk = _proj_conv(hb, in_w[:, :o_v].astype(BF16), conv_w[:, :o_v], no_bias[:o_v], S,
                    f"{tag}_qk").reshape(B, S, o_v)
    v = _proj_conv(hb, in_w[:, o_v:o_z].astype(BF16), conv_w[:, o_v:], no_bias[o_v:], S,
                   f"{tag}_v", out_dtype=BF16).reshape(B, S, GDN_VAL_DIM)
    zs = _proj_silu(hb, in_w[:, o_z:o_ba].astype(BF16), f"{tag}_z").reshape(B, S, GDN_VAL_DIM)
    ba = _matmul(hb, in_w[:, o_ba:].astype(BF16), f"{tag}_ba").reshape(B, S, 2 * GDN_N_V_HEADS)
    bp = _pad_heads(ba[..., :GDN_N_V_HEADS], HG)
    ap = _pad_heads(ba[..., GDN_N_V_HEADS:], HG)
    hp = jnp.zeros((HG, SUBLANES, LANES), F32)
    hp = hp.at[:, 0, :GDN_V_PER_STEP].set(a_log.reshape(HG, -1))
    hp = hp.at[:, 1, :GDN_V_PER_STEP].set(dt_bias.reshape(HG, -1))
    o = _gdn_core(qk, v, zs, bp, ap, hp, norm_w.reshape(1, GDN_DV), f"{tag}_core")
    hn, hnb = _outproj_ln(o.reshape(T, GDN_VAL_DIM), out_w.astype(BF16), h.reshape(T, D),
                          ln_g, ln_b, f"{tag}_out")
    return hn.reshape(B, S, D), hnb


def kernel(x, positions, ssd_in_w, ssd_conv_w, ssd_conv_b, ssd_dt_bias, ssd_a_log, ssd_d, ssd_norm_w,
           ssd_out_w, mla_in_w, mla_q_norm_w, mla_q_up_w, mla_kv_norm_w, mla_kv_up_w, mla_out_w,
           gdn_in_w, gdn_conv_w, gdn_a_log, gdn_dt_bias, gdn_norm_w, gdn_out_w, ln_g, ln_b):
    B, S, D = x.shape
    h = x
    hb = x.reshape(B * S, D).astype(BF16)
    for i in range(DEPTH):
        kind, j = i % 3, i // 3
        if kind == 0:
            h, hb = _ssd_layer(h, hb, ssd_in_w[j], ssd_conv_w[j], ssd_conv_b[j], ssd_dt_bias[j],
                               ssd_a_log[j], ssd_d[j], ssd_norm_w[j], ssd_out_w[j], ln_g[i], ln_b[i],
                               f"l{i}_ssd")
        elif kind == 1:
            h, hb = _mla_layer(h, hb, positions, mla_in_w[j], mla_q_norm_w[j], mla_q_up_w[j],
                               mla_kv_norm_w[j], mla_kv_up_w[j], mla_out_w[j], ln_g[i], ln_b[i],
                               f"l{i}_mla")
        else:
            h, hb = _gdn_layer(h, hb, gdn_in_w[j], gdn_conv_w[j], gdn_a_log[j], gdn_dt_bias[j],
                               gdn_norm_w[j], gdn_out_w[j], ln_g[i], ln_b[i], f"l{i}_gdn")
    return h
```

```python
import functools
import math

import jax
import jax.numpy as jnp
from jax import lax
from jax.experimental import pallas as pl
from jax.experimental.pallas import tpu as pltpu

F32 = jnp.float32
BF16 = jnp.bfloat16

D_MODEL = 2048
DEPTH = 4
DEEPNORM_ALPHA = (2.0 * DEPTH) ** 0.25
LN_EPS = 1e-5
RMS_EPS = 1e-6

SSD_D_INNER = 4096
SSD_HEAD_DIM = 64
SSD_N_HEADS = 64
SSD_N_GROUPS = 8
SSD_HEADS_PER_GROUP = SSD_N_HEADS // SSD_N_GROUPS
SSD_D_STATE = 128
SSD_CONV = 4
SSD_CHUNK = 128
SSD_GROUP_CH = SSD_D_INNER // SSD_N_GROUPS
SSD_BC = SSD_N_GROUPS * SSD_D_STATE
SSD_CONV_DIM = SSD_D_INNER + 2 * SSD_BC

MLA_N_HEADS = 16
MLA_Q_RANK = 768
MLA_KV_RANK = 512
MLA_NOPE = 128
MLA_ROPE = 64
MLA_V = 128
MLA_QK = MLA_NOPE + MLA_ROPE
MLA_GATE = MLA_N_HEADS * MLA_V
ROPE_THETA = 10000.0

GDN_N_QK_HEADS = 16
GDN_N_V_HEADS = 32
GDN_DK = 128
GDN_DV = 128
GDN_KEY_DIM = GDN_N_QK_HEADS * GDN_DK
GDN_VAL_DIM = GDN_N_V_HEADS * GDN_DV
GDN_CONV = 4
GDN_CHUNK = 64
GDN_CONV_DIM = 2 * GDN_KEY_DIM + GDN_VAL_DIM

LANES = 128
SUBLANES = 8
LOG2E = math.log2(math.e)
VMEM_LIMIT_BYTES = 56 * 1024 * 1024


def _cparams(sem):
    return pltpu.CompilerParams(dimension_semantics=sem, vmem_limit_bytes=VMEM_LIMIT_BYTES)


def _sigmoid(x):
    return 1.0 / (1.0 + jnp.exp(-x))


def _silu(x):
    return x * _sigmoid(x)


def _softplus(x):
    return jnp.maximum(x, 0.0) + jnp.log1p(jnp.exp(-jnp.abs(x)))


def _dot(a, b):
    return jnp.dot(a, b, preferred_element_type=F32)


def _dot_nt(a, b):
    return lax.dot_general(a, b, (((1,), (1,)), ((), ())), preferred_element_type=F32)


def _dot_tn(a, b):
    return lax.dot_general(a, b, (((0,), (0,)), ((), ())), preferred_element_type=F32)


def _dot_sel(sel_bf16, a):
    hi = a.astype(BF16)
    r1 = a - hi.astype(F32)
    mid = r1.astype(BF16)
    lo = (r1 - mid.astype(F32)).astype(BF16)
    return _dot(sel_bf16, hi) + _dot(sel_bf16, mid) + _dot(sel_bf16, lo)


def _mm_kernel(x_ref, w_ref, o_ref):
    o_ref[...] = _dot(x_ref[...], w_ref[...]).astype(o_ref.dtype)


def _pick_tile(n, pref):
    t = min(n, pref)
    while n % t:
        t -= LANES
    return t


def _matmul(x, w, name, out_dtype=F32, tm=1024, tn=512):
    T, K = x.shape
    N = w.shape[1]
    tm = min(tm, T)
    tn = _pick_tile(N, tn)
    return pl.pallas_call(
        _mm_kernel,
        grid=(T // tm, N // tn),
        in_specs=[pl.BlockSpec((tm, K), lambda i, j: (i, 0)),
                  pl.BlockSpec((K, tn), lambda i, j: (0, j))],
        out_specs=pl.BlockSpec((tm, tn), lambda i, j: (i, j)),
        out_shape=jax.ShapeDtypeStruct((T, N), out_dtype),
        compiler_params=_cparams(("parallel", "parallel")),
        name=name,
    )(x, w)


def _outproj_ln_kernel(x_ref, w_ref, h_ref, g_ref, b_ref, o_ref, ob_ref):
    k = pl.program_id(1)

    @pl.when(k == 0)
    def _():
        o_ref[...] = jnp.zeros_like(o_ref)

    o_ref[...] += _dot(x_ref[...], w_ref[...])

    @pl.when(k == pl.num_programs(1) - 1)
    def _():
        r = DEEPNORM_ALPHA * h_ref[...] + o_ref[...]
        mu = jnp.mean(r, axis=-1, keepdims=True)
        d = r - mu
        var = jnp.mean(d * d, axis=-1, keepdims=True)
        out = d * lax.rsqrt(var + LN_EPS) * g_ref[...] + b_ref[...]
        o_ref[...] = out
        ob_ref[...] = out.astype(BF16)


def _outproj_ln(x, w, h, g, b, name, tm=1024, tk=512):
    T, K = x.shape
    N = w.shape[1]
    tm = min(tm, T)
    tk = min(tk, K)
    return pl.pallas_call(
        _outproj_ln_kernel,
        grid=(T // tm, K // tk),
        in_specs=[pl.BlockSpec((tm, tk), lambda i, k: (i, k)),
                  pl.BlockSpec((tk, N), lambda i, k: (k, 0)),
                  pl.BlockSpec((tm, N), lambda i, k: (i, 0)),
                  pl.BlockSpec((1, N), lambda i, k: (0, 0)),
                  pl.BlockSpec((1, N), lambda i, k: (0, 0))],
        out_specs=[pl.BlockSpec((tm, N), lambda i, k: (i, 0)),
                   pl.BlockSpec((tm, N), lambda i, k: (i, 0))],
        out_shape=[jax.ShapeDtypeStruct((T, N), F32), jax.ShapeDtypeStruct((T, N), BF16)],
        compiler_params=_cparams(("parallel", "arbitrary")),
        name=name,
    )(x, w, h, g.reshape(1, N), b.reshape(1, N))


MXU_COLS = 256


def _column_pieces(x_ref, w_ref):
    x = x_ref[...]
    starts = list(range(0, w_ref.shape[1], MXU_COLS))
    nxt = _dot(x, w_ref[:, :MXU_COLS])
    for p, c in enumerate(starts):
        cur = nxt
        if p + 1 < len(starts):
            nxt = _dot(x, w_ref[:, starts[p + 1]:starts[p + 1] + MXU_COLS])
        yield c, cur


def _proj_silu_kernel(x_ref, w_ref, o_ref):
    for c, y in _column_pieces(x_ref, w_ref):
        o_ref[:, c:c + MXU_COLS] = _silu(y).astype(o_ref.dtype)


def _proj_conv_kernel(x_ref, w_ref, cw_ref, cb_ref, o_ref, halo_ref, *, taps, tiles_per_seq):
    i, j = pl.program_id(0), pl.program_id(1)
    tm = x_ref.shape[0]

    @pl.when(i % tiles_per_seq == 0)
    def _():
        halo_ref[j] = jnp.zeros(halo_ref.shape[1:], F32)

    for c, y in _column_pieces(x_ref, w_ref):
        cs = slice(c, c + MXU_COLS)
        acc = cb_ref[:, cs] + cw_ref[taps - 1:taps, cs] * y
        for d in range(1, taps):
            acc = acc + cw_ref[taps - 1 - d:taps - d, cs] * pltpu.roll(y, d, 0)
        o_ref[:, cs] = _silu(acc).astype(o_ref.dtype)
        head = jnp.concatenate([halo_ref[j, :, cs], y[:SUBLANES, :]], axis=0)
        halo_ref[j, :, cs] = y[tm - SUBLANES:, :]
        acc = jnp.broadcast_to(cb_ref[:, cs], (SUBLANES, MXU_COLS))
        for k in range(taps):
            off = SUBLANES - (taps - 1) + k
            acc = acc + cw_ref[k:k + 1, cs] * head[off:off + SUBLANES]
        o_ref[:SUBLANES, cs] = _silu(acc).astype(o_ref.dtype)


def _col_tile(c0, n, pref):
    assert c0 % LANES == 0 and n % LANES == 0
    tn = min(n, pref)
    while n % tn or c0 % tn:
        tn -= LANES
    return tn


def _proj_silu(x, w, c0, n, name, out_dtype=F32, tm=1024, tn=1024):
    T, K = x.shape
    tm = min(tm, T)
    tn = _col_tile(c0, n, tn)
    j0 = c0 // tn
    return pl.pallas_call(
        _proj_silu_kernel,
        grid=(T // tm, n // tn),
        in_specs=[pl.BlockSpec((tm, K), lambda i, j: (i, 0)),
                  pl.BlockSpec((K, tn), lambda i, j: (0, j0 + j))],
        out_specs=pl.BlockSpec((tm, tn), lambda i, j: (i, j)),
        out_shape=jax.ShapeDtypeStruct((T, n), out_dtype),
        compiler_params=_cparams(("parallel", "parallel")),
        name=name,
    )(x, w)


def _proj_conv(x, w, c0, n, conv_w, conv_b, cc0, seq_len, name, out_dtype=F32, tm=1024, tn=1024):
    T, K = x.shape
    taps = conv_w.shape[0]
    tm = min(tm, seq_len)
    assert seq_len % tm == 0
    tn = _col_tile(c0, n, tn)
    assert cc0 % tn == 0
    j0, jc0 = c0 // tn, cc0 // tn
    return pl.pallas_call(
        functools.partial(_proj_conv_kernel, taps=taps, tiles_per_seq=seq_len // tm),
        grid=(T // tm, n // tn),
        in_specs=[pl.BlockSpec((tm, K), lambda i, j: (i, 0)),
                  pl.BlockSpec((K, tn), lambda i, j: (0, j0 + j)),
                  pl.BlockSpec((taps, tn), lambda i, j: (0, jc0 + j)),
                  pl.BlockSpec((1, tn), lambda i, j: (0, jc0 + j))],
        out_specs=pl.BlockSpec((tm, tn), lambda i, j: (i, j)),
        out_shape=jax.ShapeDtypeStruct((T, n), out_dtype),
        scratch_shapes=[pltpu.VMEM((n // tn, SUBLANES, tn), F32)],
        compiler_params=_cparams(("arbitrary", "arbitrary")),
        name=name,
    )(x, w, conv_w, conv_b.reshape(1, -1))


def _ssd_kernel(x_ref, b_ref, c_ref, z_ref, dt_ref, hp_ref, d_ref, nw_ref, o_ref,
                state_ref, y_ref):
    L, N, P = SSD_CHUNK, SSD_D_STATE, SSD_HEAD_DIM
    G, GC, HPG = SSD_N_GROUPS, SSD_GROUP_CH, SSD_HEADS_PER_GROUP
    ci = pl.program_id(1)

    @pl.when(ci == 0)
    def _():
        state_ref[...] = jnp.zeros_like(state_ref)

    lane = lax.broadcasted_iota(jnp.int32, (1, LANES), 1)
    rows = lax.broadcasted_iota(jnp.int32, (L, L), 0)
    cols = lax.broadcasted_iota(jnp.int32, (L, L), 1)
    causal = rows >= cols
    tri = causal.astype(BF16)
    lo_half = lane < P

    dt = _softplus(dt_ref[0] + hp_ref[0:1, :])
    neg_a = jnp.where(lane < SSD_N_HEADS, -jnp.exp(hp_ref[1:2, :]), 0.0)
    a_cs = _dot_sel(tri, dt * neg_a) * LOG2E
    a_cs_t = a_cs.T
    dt_t = dt.T
    a_last = a_cs[L - 1:L, :]
    state_w = dt * jnp.exp2(a_last - a_cs)
    chunk_decay = jnp.exp2(a_last)

    def pair_cols(v, h0):
        return jnp.where(lo_half, jnp.broadcast_to(v[:, h0:h0 + 1], (v.shape[0], LANES)),
                         jnp.broadcast_to(v[:, h0 + 1:h0 + 2], (v.shape[0], LANES)))

    cb, y_off, bgt = [], [], []
    for g in range(G):
        bgb = b_ref[0, :, g * N:(g + 1) * N]
        cgb = c_ref[0, :, g * N:(g + 1) * N]
        cb.append(_dot_nt(cgb, bgb))
        y_off.append(_dot(cgb, state_ref[g].astype(BF16)))
        bgt.append(bgb.astype(F32).T.astype(BF16))

    for g in range(G):
        xs = []
        for j in range(HPG // 2):
            h0 = g * HPG + 2 * j
            sl = slice(g * GC + j * LANES, g * GC + (j + 1) * LANES)
            xp = x_ref[0, :, sl]
            xb = xp.astype(BF16)
            zero = jnp.zeros_like(xb)
            m = []
            for h in (h0, h0 + 1):
                col = jnp.broadcast_to(a_cs[:, h:h + 1], (L, L))
                decay = jnp.exp2(jnp.where(causal, col - a_cs_t[h:h + 1, :], -jnp.inf))
                m.append((cb[g] * decay * dt_t[h:h + 1, :]).astype(BF16))
            y = _dot(jnp.concatenate(m, axis=1),
                     jnp.concatenate([jnp.where(lo_half, xb, zero), jnp.where(lo_half, zero, xb)], axis=0))
            y = y + y_off[g][:, j * LANES:(j + 1) * LANES] * jnp.exp2(pair_cols(a_cs, h0))
            y = y + xp * d_ref[:, sl]
            y_ref[:, sl] = y * z_ref[0, :, sl]
            xs.append((xp * pair_cols(state_w, h0)).astype(BF16))
        cd = jnp.concatenate([pair_cols(chunk_decay, g * HPG + 2 * j) for j in range(HPG // 2)], axis=1)
        state_ref[g] = state_ref[g] * cd + _dot(bgt[g], jnp.concatenate(xs, axis=1))

    for g in range(G):
        gsl = slice(g * GC, (g + 1) * GC)
        yg = y_ref[:, gsl]
        ms = jnp.sum(yg * yg, axis=-1, keepdims=True) * (1.0 / GC)
        o_ref[0, :, gsl] = (yg * lax.rsqrt(ms + RMS_EPS) * nw_ref[:, gsl]).astype(o_ref.dtype)


def _ssd_scan(xs, bc, zs, dtp, hp, dskip, normw, name):
    B, S, _ = xs.shape
    L, G, N, DI = SSD_CHUNK, SSD_N_GROUPS, SSD_D_STATE, SSD_D_INNER
    nc = S // L
    return pl.pallas_call(
        _ssd_kernel,
        grid=(B, nc),
        in_specs=[pl.BlockSpec((1, L, DI), lambda b, c: (b, c, 0)),
                  pl.BlockSpec((1, L, SSD_BC), lambda b, c: (b, c, 0)),
                  pl.BlockSpec((1, L, SSD_BC), lambda b, c: (b, c, 1)),
                  pl.BlockSpec((1, L, DI), lambda b, c: (b, c, 0)),
                  pl.BlockSpec((1, L, LANES), lambda b, c: (b, c, 0)),
                  pl.BlockSpec((SUBLANES, LANES), lambda b, c: (0, 0)),
                  pl.BlockSpec((1, DI), lambda b, c: (0, 0)),
                  pl.BlockSpec((1, DI), lambda b, c: (0, 0))],
        out_specs=pl.BlockSpec((1, L, DI), lambda b, c: (b, c, 0)),
        out_shape=jax.ShapeDtypeStruct((B, S, DI), BF16),
        scratch_shapes=[pltpu.VMEM((G, N, SSD_GROUP_CH), F32), pltpu.VMEM((L, DI), F32)],
        compiler_params=_cparams(("parallel", "arbitrary")),
        name=name,
    )(xs, bc, bc, zs, dtp, hp, dskip, normw)


def _pad_heads(a, groups):
    per = a.shape[-1] // groups
    a = a.reshape(a.shape[:-1] + (groups, per))
    a = jnp.pad(a, [(0, 0)] * (a.ndim - 1) + [(0, LANES - per)])
    return a.reshape(a.shape[:-2] + (groups * LANES,))


def _ssd_layer(h, hb, in_w, conv_w, conv_b, dt_bias, a_log, d_skip, norm_w, out_w, ln_g, ln_b, tag):
    B, S, D = h.shape
    T = B * S
    DI = SSD_D_INNER
    o_x, o_bc, o_dt = DI, 2 * DI, DI + SSD_CONV_DIM
    wb = in_w.astype(BF16)
    w_dt = jnp.pad(wb[:, o_dt:], ((0, 0), (0, LANES - SSD_N_HEADS)))
    zs = _proj_silu(hb, wb, 0, DI, f"{tag}_z").reshape(B, S, DI)
    xs = _proj_conv(hb, wb, o_x, DI, conv_w, conv_b, 0, S, f"{tag}_x").reshape(B, S, DI)
    bc = _proj_conv(hb, wb, o_bc, 2 * SSD_BC, conv_w, conv_b, DI, S, f"{tag}_bc",
                    out_dtype=BF16).reshape(B, S, 2 * SSD_BC)
    dtp = _matmul(hb, w_dt, f"{tag}_dt").reshape(B, S, LANES)
    hp = jnp.zeros((SUBLANES, LANES), F32)
    hp = hp.at[0, :SSD_N_HEADS].set(dt_bias)
    hp = hp.at[1, :SSD_N_HEADS].set(a_log)
    dskip = jnp.repeat(d_skip, SSD_HEAD_DIM).reshape(1, DI)
    y = _ssd_scan(xs, bc, zs, dtp, hp, dskip, norm_w.reshape(1, DI), f"{tag}_scan")
    hn, hnb = _outproj_ln(y.reshape(T, SSD_D_INNER), out_w.astype(BF16), h.reshape(T, D),
                          ln_g, ln_b, f"{tag}_out")
    return hn.reshape(B, S, D), hnb


MLA_Q_SCALE = MLA_QK ** -0.5 * math.log2(math.e)


def _rope_table(pos_ref, invf_ref):
    lane = lax.broadcasted_iota(jnp.int32, (1, LANES), 1)
    ang = pos_ref[0].astype(F32) * invf_ref[...]
    q = MLA_ROPE // 2
    return jnp.where(lane < 2 * q, jnp.cos(ang), jnp.where(lane < 3 * q, -1.0, 1.0) * jnp.sin(ang))


def _rope_apply(xr, table):
    prod = xr * table
    return prod + pltpu.roll(prod, MLA_ROPE, 1)


def _rms_bf16(x, w):
    ms = jnp.mean(x * x, axis=-1, keepdims=True)
    return (x * lax.rsqrt(ms + RMS_EPS) * w).astype(BF16)


def _mla_q_kernel(qc_ref, nw_ref, w_ref, pos_ref, invf_ref, o_ref):
    xn = _rms_bf16(qc_ref[0], nw_ref[...])
    table = _rope_table(pos_ref, invf_ref)
    hw = MLA_NOPE + 2 * MLA_ROPE
    for hd in range(MLA_N_HEADS):
        r = _dot(xn, w_ref[:, hd * hw:(hd + 1) * hw])
        roped = _rope_apply(r[:, MLA_NOPE:], table)
        o_ref[0, hd, :, :MLA_NOPE] = (r[:, :MLA_NOPE] * MLA_Q_SCALE).astype(BF16)
        o_ref[0, hd, :, MLA_NOPE:] = (roped[:, :MLA_ROPE] * MLA_Q_SCALE).astype(BF16)


def _mla_kv_kernel(kvc_ref, nw_ref, w_ref, kr_ref, pos_ref, invf_ref, k_ref, v_ref):
    xn = _rms_bf16(kvc_ref[0], nw_ref[...])
    k_rope = _rope_apply(kr_ref[0], _rope_table(pos_ref, invf_ref))[:, :MLA_ROPE].astype(BF16)
    hw = MLA_NOPE + MLA_V
    for hd in range(MLA_N_HEADS):
        r = _dot(xn, w_ref[:, hd * hw:(hd + 1) * hw])
        k_ref[0, hd, :, :MLA_NOPE] = r[:, :MLA_NOPE].astype(BF16)
        k_ref[0, hd, :, MLA_NOPE:] = k_rope
        v_ref[0, hd] = r[:, MLA_NOPE:].astype(BF16)


def _mla_attn_kernel(q_ref, k_ref, v_ref, z_ref, o_ref, *, tk):
    qi = pl.program_id(2)
    halves = (q_ref[0, 0, :tk, :], q_ref[0, 0, tk:, :])
    rows = lax.broadcasted_iota(jnp.int32, (tk, tk), 0)
    cols = lax.broadcasted_iota(jnp.int32, (tk, tk), 1)

    def block(kb, carry, masked):
        start = pl.multiple_of(kb * tk, tk)
        k = k_ref[0, 0, pl.ds(start, tk), :]
        v = v_ref[0, 0, pl.ds(start, tk), :]
        live = [e for e in range(2) if masked[e] is not None]
        s = {e: _dot_nt(halves[e], k) for e in live}
        out = list(carry)
        for e in live:
            m, l, acc = carry[e]
            se = jnp.where(rows >= cols, s[e], -jnp.inf) if masked[e] else s[e]
            m_new = jnp.maximum(m, jnp.max(se, axis=-1, keepdims=True))
            p = jnp.exp2(se - m_new)
            alpha = jnp.exp2(m - m_new)
            l = alpha * l + jnp.sum(p, axis=-1, keepdims=True)
            out[e] = (m_new, l, alpha * acc + _dot(p.astype(BF16), v))
        return tuple(out)

    init = tuple((jnp.full((tk, 1), -jnp.inf, F32), jnp.zeros((tk, 1), F32), jnp.zeros((tk, MLA_V), F32))
                 for _ in range(2))
    carry = lax.fori_loop(0, 2 * qi, lambda kb, c: block(kb, c, (False, False)), init)
    carry = block(2 * qi, carry, (True, False))
    carry = block(2 * qi + 1, carry, (None, True))
    for e in range(2):
        _, l, acc = carry[e]
        rs = slice(e * tk, (e + 1) * tk)
        o_ref[0, rs, :] = (acc / l * z_ref[0, rs, :]).astype(o_ref.dtype)


def _mla_layer(h, hb, positions, in_w, q_norm_w, q_up_w, kv_norm_w, kv_up_w, out_w, ln_g, ln_b, tag):
    B, S, D = h.shape
    T = B * S
    H = MLA_N_HEADS
    o_kv = MLA_Q_RANK
    o_kr = o_kv + MLA_KV_RANK
    o_z = o_kr + MLA_ROPE
    half = MLA_ROPE // 2
    wb = in_w.astype(BF16)
    w_kr = wb[:, o_kr:o_z]
    w_kr2 = jnp.concatenate([w_kr, w_kr[:, half:], w_kr[:, :half]], axis=1)
    q_c = _matmul(hb, wb[:, :o_kv], f"{tag}_qc", tn=768).reshape(B, S, MLA_Q_RANK)
    kv_c = _matmul(hb, wb[:, o_kv:o_kr], f"{tag}_kvc").reshape(B, S, MLA_KV_RANK)
    kr = _matmul(hb, w_kr2, f"{tag}_kr").reshape(B, S, LANES)
    z = _proj_silu(hb, wb[:, o_z:], 0, MLA_GATE, f"{tag}_z").reshape(B, S, MLA_GATE)

    wq = q_up_w.reshape(MLA_Q_RANK, H, MLA_QK)
    wq_r = wq[:, :, MLA_NOPE:]
    wq3 = jnp.concatenate([wq, wq_r[:, :, half:], wq_r[:, :, :half]], axis=2)
    wq3 = wq3.reshape(MLA_Q_RANK, H * (MLA_QK + MLA_ROPE)).astype(BF16)
    wkv = kv_up_w.astype(BF16)

    inv_freq = ROPE_THETA ** (-jnp.arange(0, MLA_ROPE, 2, dtype=F32) / MLA_ROPE)
    invf = jnp.tile(inv_freq, LANES // half).reshape(1, LANES)
    pos3 = positions.reshape(B, S, 1)

    tm = min(512, S)
    qf = pl.pallas_call(
        _mla_q_kernel,
        grid=(B, S // tm),
        in_specs=[pl.BlockSpec((1, tm, MLA_Q_RANK), lambda b, i: (b, i, 0)),
                  pl.BlockSpec((1, MLA_Q_RANK), lambda b, i: (0, 0)),
                  pl.BlockSpec(wq3.shape, lambda b, i: (0, 0)),
                  pl.BlockSpec((1, tm, 1), lambda b, i: (b, i, 0)),
                  pl.BlockSpec((1, LANES), lambda b, i: (0, 0))],
        out_specs=pl.BlockSpec((1, H, tm, MLA_QK), lambda b, i: (b, 0, i, 0)),
        out_shape=jax.ShapeDtypeStruct((B, H, S, MLA_QK), BF16),
        compiler_params=_cparams(("parallel", "parallel")),
        name=f"{tag}_qup",
    )(q_c, q_norm_w.reshape(1, -1), wq3, pos3, invf)

    kf, vf = pl.pallas_call(
        _mla_kv_kernel,
        grid=(B, S // tm),
        in_specs=[pl.BlockSpec((1, tm, MLA_KV_RANK), lambda b, i: (b, i, 0)),
                  pl.BlockSpec((1, MLA_KV_RANK), lambda b, i: (0, 0)),
                  pl.BlockSpec(wkv.shape, lambda b, i: (0, 0)),
                  pl.BlockSpec((1, tm, LANES), lambda b, i: (b, i, 0)),
                  pl.BlockSpec((1, tm, 1), lambda b, i: (b, i, 0)),
                  pl.BlockSpec((1, LANES), lambda b, i: (0, 0))],
        out_specs=[pl.BlockSpec((1, H, tm, MLA_QK), lambda b, i: (b, 0, i, 0)),
                   pl.BlockSpec((1, H, tm, MLA_V), lambda b, i: (b, 0, i, 0))],
        out_shape=[jax.ShapeDtypeStruct((B, H, S, MLA_QK), BF16),
                   jax.ShapeDtypeStruct((B, H, S, MLA_V), BF16)],
        compiler_params=_cparams(("parallel", "parallel")),
        name=f"{tag}_kvup",
    )(kv_c, kv_norm_w.reshape(1, -1), wkv, kr, pos3, invf)

    tq = min(1024, S)
    o = pl.pallas_call(
        functools.partial(_mla_attn_kernel, tk=tq // 2),
        grid=(B, H, S // tq),
        in_specs=[pl.BlockSpec((1, 1, tq, MLA_QK), lambda b, hd, i: (b, hd, i, 0)),
                  pl.BlockSpec((1, 1, S, MLA_QK), lambda b, hd, i: (b, hd, 0, 0)),
                  pl.BlockSpec((1, 1, S, MLA_V), lambda b, hd, i: (b, hd, 0, 0)),
                  pl.BlockSpec((1, tq, MLA_V), lambda b, hd, i: (b, i, hd))],
        out_specs=pl.BlockSpec((1, tq, MLA_V), lambda b, hd, i: (b, i, hd)),
        out_shape=jax.ShapeDtypeStruct((B, S, MLA_GATE), BF16),
        compiler_params=_cparams(("parallel", "parallel", "arbitrary")),
        name=f"{tag}_attn",
    )(qf, kf, vf, z)

    hn, hnb = _outproj_ln(o.reshape(T, MLA_GATE), out_w.astype(BF16), h.reshape(T, D),
                          ln_g, ln_b, f"{tag}_out")
    return hn.reshape(B, S, D), hnb


GDN_QK_PER_STEP = 4
GDN_V_PER_STEP = 2 * GDN_QK_PER_STEP
GDN_CHUNKS_PER_STEP = 4
GDN_ROWS = GDN_CHUNKS_PER_STEP * GDN_CHUNK


def _block_mask(nrows, ncols, rblk, cblk):
    r = lax.broadcasted_iota(jnp.int32, (nrows, ncols), 0) // rblk
    c = lax.broadcasted_iota(jnp.int32, (nrows, ncols), 1) // cblk
    return jnp.where(r == c, 1.0, 0.0).astype(BF16)


def _block_diag(x, reps, mask):
    return jnp.concatenate([x] * reps, axis=0) * mask


def _place_blocks(x, layout):
    zero = jnp.zeros((x.shape[0], LANES), x.dtype)
    rows = [jnp.concatenate([zero if b is None else x[:, b * LANES:(b + 1) * LANES] for b in row], axis=1)
            for row in layout]
    return jnp.concatenate(rows, axis=0)


def _gdn_kernel(q_ref, k_ref, v_ref, z_ref, b_ref, a_ref, hp_ref, nw_ref, o_ref,
                state_ref, u_s, wq_s, kd_s, qkm_s, egl_s, qn_s, kn_s):
    L, R, C = GDN_CHUNK, GDN_ROWS, GDN_CHUNKS_PER_STEP
    NQ, NV = GDN_QK_PER_STEP, GDN_V_PER_STEP
    W = NV * L
    ri = pl.program_id(2)

    @pl.when(ri == 0)
    def _():
        state_ref[...] = jnp.zeros_like(state_ref)

    beta = _sigmoid(b_ref[0])
    g = -jnp.exp(hp_ref[0, 0:1, :]) * _softplus(a_ref[0] + hp_ref[0, 1:2, :])
    rr = lax.broadcasted_iota(jnp.int32, (R, R), 0)
    cc = lax.broadcasted_iota(jnp.int32, (R, R), 1)
    blocktri = ((rr >= cc) & (rr // L == cc // L)).astype(BF16)
    gcs = _dot_sel(blocktri, g) * LOG2E
    gcs_t = gcs.T
    beta_t = beta.T

    lane = lax.broadcasted_iota(jnp.int32, (1, LANES), 1)
    lo = lane < L
    rows_w = lax.broadcasted_iota(jnp.int32, (L, W), 0)
    s_w = lax.broadcasted_iota(jnp.int32, (L, W), 1) % L
    incl_w = rows_w >= s_w
    strict_w = rows_w > s_w
    eye_w = jnp.where(rows_w == s_w, 1.0, 0.0)
    head_sel = (lax.broadcasted_iota(jnp.int32, (NV, W), 0)
                == lax.broadcasted_iota(jnp.int32, (NV, W), 1) // L)
    m_p = _block_mask(4 * L, 4 * L, L, L)
    kk_layout = [[hv // 2 if cb == hv // 2 else None for cb in range(NQ)] for hv in range(NV)]
    uw_layout = [[0, None, 2, None], [None, 1, None, 3]]
    pair_layout = [[0, None], [None, 1]]
    nw = nw_ref[...]

    def row_vec(t_heads, c):
        x = t_heads[0:NV, c * L:(c + 1) * L]
        x8 = jnp.concatenate([x] * NV, axis=1)
        return jnp.sum(jnp.where(head_sel, x8, 0.0), axis=0, keepdims=True)

    def col_wide(cols):
        return jnp.concatenate([jnp.where(lo, cols[2 * j], cols[2 * j + 1]) for j in range(NV // 2)], axis=1)

    chunks = range(C)
    rows_of = [slice(c * L, (c + 1) * L) for c in chunks]
    kq = []
    for c in chunks:
        qn, kn = [], []
        for i in range(NQ):
            q = q_ref[0, rows_of[c], i * GDN_DK:(i + 1) * GDN_DK]
            k = k_ref[0, rows_of[c], i * GDN_DK:(i + 1) * GDN_DK]
            qn.append(q * lax.rsqrt(jnp.sum(q * q, axis=-1, keepdims=True) + RMS_EPS) * (GDN_DK ** -0.5))
            kn.append(k * lax.rsqrt(jnp.sum(k * k, axis=-1, keepdims=True) + RMS_EPS))
        qn_s[c] = jnp.concatenate(qn, axis=1)
        kn_s[c] = jnp.concatenate(kn, axis=1)
        qb4 = jnp.concatenate([x.astype(BF16) for x in qn], axis=1)
        kb4 = jnp.concatenate([x.astype(BF16) for x in kn], axis=1)
        kq.append(_dot_nt(jnp.concatenate([qb4, kb4], axis=0), _place_blocks(kb4, kk_layout)))

    p_cur, t_cur = {}, {}
    for c in chunks:
        sl = rows_of[c]
        colg = [jnp.broadcast_to(gcs[sl, h:h + 1], (L, LANES)) for h in range(NV)]
        colb = [jnp.broadcast_to(beta[sl, h:h + 1], (L, LANES)) for h in range(NV)]
        decay = jnp.exp2(jnp.where(incl_w, col_wide(colg) - row_vec(gcs_t, c), -jnp.inf))
        a_w = jnp.where(strict_w, kq[c][L:] * col_wide(colb) * decay, 0.0)
        qkm_s[c] = (kq[c][:L] * decay).astype(BF16)
        for gi in range(NV // 4):
            gs = slice(gi * 4 * L, (gi + 1) * 4 * L)
            t_cur[c, gi] = eye_w[:, gs] - a_w[:, gs]
            pb = a_w[:, gs].astype(BF16)
            p_cur[c, gi] = _dot(pb, _block_diag(pb, 4, m_p))

    for it in range(5):
        for key in sorted(p_cur):
            pb = p_cur[key].astype(BF16)
            bd = _block_diag(pb, 4, m_p)
            t = t_cur[key]
            if it < 4:
                rr2 = _dot(jnp.concatenate([pb, t.astype(BF16)], axis=0), bd)
                p_cur[key] = rr2[:L]
                t_cur[key] = t + rr2[L:]
            else:
                t_cur[key] = t + _dot(t.astype(BF16), bd)

    for c in chunks:
        sl = rows_of[c]
        t_w = jnp.concatenate([t_cur[c, gi] for gi in range(NV // 4)], axis=1)
        tb_w = (t_w * row_vec(beta_t, c)).astype(BF16)
        colg = [jnp.broadcast_to(gcs[sl, h:h + 1], (L, LANES)) for h in range(NV)]
        eg = [jnp.exp2(x) for x in colg]
        gl = [gcs[c * L + L - 1:c * L + L, h:h + 1] for h in range(NV)]
        for i in range(NQ):
            h0, h1 = 2 * i, 2 * i + 1
            psl = slice(i * 2 * GDN_DV, (i + 1) * 2 * GDN_DV)
            qni = qn_s[c, :, i * GDN_DK:(i + 1) * GDN_DK]
            kni = kn_s[c, :, i * GDN_DK:(i + 1) * GDN_DK]
            base = jnp.concatenate([v_ref[0, sl, psl].astype(BF16),
                                    (kni * eg[h0]).astype(BF16), (kni * eg[h1]).astype(BF16)], axis=1)
            uw = _dot(tb_w[:, i * 2 * L:(i + 1) * 2 * L], _place_blocks(base, uw_layout))
            u_s[c, :, psl] = uw[:, :2 * GDN_DV]
            qd = jnp.concatenate([(qni * eg[h0]).astype(BF16), (qni * eg[h1]).astype(BF16)], axis=1)
            wq_s[c, i] = jnp.concatenate([uw[:, 2 * GDN_DV:].astype(BF16), qd], axis=0)
            kd_s[c, i] = jnp.concatenate([(kni * jnp.exp2(gl[h0] - colg[h0])).astype(BF16),
                                          (kni * jnp.exp2(gl[h1] - colg[h1])).astype(BF16)], axis=0)
            egl_s[c, i] = jnp.concatenate([jnp.broadcast_to(jnp.exp2(gl[h0]), (SUBLANES, GDN_DV)),
                                           jnp.broadcast_to(jnp.exp2(gl[h1]), (SUBLANES, GDN_DV))], axis=1)

    pairs = range(NQ)
    for c in chunks:
        sl = rows_of[c]
        st = [state_ref[i] for i in pairs]
        r1 = [_dot(wq_s[c, i], _place_blocks(st[i].astype(BF16), pair_layout)) for i in pairs]
        bdv = []
        for i in pairs:
            psl = slice(i * 2 * GDN_DV, (i + 1) * 2 * GDN_DV)
            vb = (u_s[c, :, psl] - r1[i][:L]).astype(BF16)
            bdv.append(_place_blocks(vb, pair_layout))
        for i in pairs:
            state_ref[i] = st[i] * egl_s[c, i, 0:1, :] + _dot_tn(kd_s[c, i], bdv[i])
        for i in pairs:
            o = r1[i][L:] + _dot(qkm_s[c, :, i * 2 * L:(i + 1) * 2 * L], bdv[i])
            for e in range(2):
                oh = o[:, e * GDN_DV:(e + 1) * GDN_DV]
                on = oh * lax.rsqrt(jnp.mean(oh * oh, axis=-1, keepdims=True) + RMS_EPS) * nw
                hsl = slice((2 * i + e) * GDN_DV, (2 * i + e + 1) * GDN_DV)
                o_ref[0, sl, hsl] = (on * z_ref[0, sl, hsl]).astype(o_ref.dtype)


def _gdn_core(qk, v, zs, bp, ap, hp, normw, name):
    B, S, _ = qk.shape
    R = min(GDN_ROWS, S)
    assert R == GDN_ROWS
    HG = GDN_N_QK_HEADS // GDN_QK_PER_STEP
    wq = GDN_QK_PER_STEP * GDN_DK
    wv = GDN_V_PER_STEP * GDN_DV
    k_blk0 = GDN_KEY_DIM // wq
    return pl.pallas_call(
        _gdn_kernel,
        grid=(B, HG, S // R),
        in_specs=[pl.BlockSpec((1, R, wq), lambda b, g, r: (b, r, g)),
                  pl.BlockSpec((1, R, wq), lambda b, g, r: (b, r, k_blk0 + g)),
                  pl.BlockSpec((1, R, wv), lambda b, g, r: (b, r, g)),
                  pl.BlockSpec((1, R, wv), lambda b, g, r: (b, r, g)),
                  pl.BlockSpec((1, R, LANES), lambda b, g, r: (b, r, g)),
                  pl.BlockSpec((1, R, LANES), lambda b, g, r: (b, r, g)),
                  pl.BlockSpec((1, SUBLANES, LANES), lambda b, g, r: (g, 0, 0)),
                  pl.BlockSpec((1, GDN_DV), lambda b, g, r: (0, 0))],
        out_specs=pl.BlockSpec((1, R, wv), lambda b, g, r: (b, r, g)),
        out_shape=jax.ShapeDtypeStruct((B, S, GDN_VAL_DIM), BF16),
        scratch_shapes=[pltpu.VMEM((GDN_QK_PER_STEP, GDN_DK, 2 * GDN_DV), F32),
                        pltpu.VMEM((GDN_CHUNKS_PER_STEP, GDN_CHUNK, wv), F32),
                        pltpu.VMEM((GDN_CHUNKS_PER_STEP, GDN_QK_PER_STEP, 2 * GDN_CHUNK, 2 * GDN_DV), BF16),
                        pltpu.VMEM((GDN_CHUNKS_PER_STEP, GDN_QK_PER_STEP, 2 * GDN_CHUNK, GDN_DK), BF16),
                        pltpu.VMEM((GDN_CHUNKS_PER_STEP, GDN_CHUNK, GDN_V_PER_STEP * GDN_CHUNK), BF16),
                        pltpu.VMEM((GDN_CHUNKS_PER_STEP, GDN_QK_PER_STEP, SUBLANES, 2 * GDN_DV), F32),
                        pltpu.VMEM((GDN_CHUNKS_PER_STEP, GDN_CHUNK, wq), F32),
                        pltpu.VMEM((GDN_CHUNKS_PER_STEP, GDN_CHUNK, wq), F32)],
        compiler_params=_cparams(("parallel", "parallel", "arbitrary")),
        name=name,
    )(qk, qk, v, zs, bp, ap, hp, normw)


def _gdn_layer(h, hb, in_w, conv_w, a_log, dt_bias, norm_w, out_w, ln_g, ln_b, tag):
    B, S, D = h.shape
    T = B * S
    o_v, o_z, o_ba = 2 * GDN_KEY_DIM, GDN_CONV_DIM, GDN_CONV_DIM + GDN_VAL_DIM
    HG = GDN_N_QK_HEADS // GDN_QK_PER_STEP
    no_bias = jnp.zeros((GDN_CONV_DIM,), F32)
    wb = in_w.astype(BF16)
    qk = _proj_conv(hb, wb, 0, o_v, conv_w, no_bias, 0, S, f"{tag}_qk").reshape(B, S, o_v)
    v = _proj_conv(hb, wb, o_v, GDN_VAL_DIM, conv_w, no_bias, o_v, S, f"{tag}_v",
                   out_dtype=BF16).reshape(B, S, GDN_VAL_DIM)
    zs = _proj_silu(hb, wb, o_z, GDN_VAL_DIM, f"{tag}_z").reshape(B, S, GDN_VAL_DIM)
    ba = _matmul(hb, wb[:, o_ba:], f"{tag}_ba").reshape(B, S, 2 * GDN_N_V_HEADS)
    bp = _pad_heads(ba[..., :GDN_N_V_HEADS], HG)
    ap = _pad_heads(ba[..., GDN_N_V_HEADS:], HG)
    hp = jnp.zeros((HG, SUBLANES, LANES), F32)
    hp = hp.at[:, 0, :GDN_V_PER_STEP].set(a_log.reshape(HG, -1))
    hp = hp.at[:, 1, :GDN_V_PER_STEP].set(dt_bias.reshape(HG, -1))
    o = _gdn_core(qk, v, zs, bp, ap, hp, norm_w.reshape(1, GDN_DV), f"{tag}_core")
    hn, hnb = _outproj_ln(o.reshape(T, GDN_VAL_DIM), out_w.astype(BF16), h.reshape(T, D),
                          ln_g, ln_b, f"{tag}_out")
    return hn.reshape(B, S, D), hnb


def kernel(x, positions, ssd_in_w, ssd_conv_w, ssd_conv_b, ssd_dt_bias, ssd_a_log, ssd_d, ssd_norm_w,
           ssd_out_w, mla_in_w, mla_q_norm_w, mla_q_up_w, mla_kv_norm_w, mla_kv_up_w, mla_out_w,
           gdn_in_w, gdn_conv_w, gdn_a_log, gdn_dt_bias, gdn_norm_w, gdn_out_w, ln_g, ln_b):
    B, S, D = x.shape
    h = x
    hb = x.reshape(B * S, D).astype(BF16)
    for i in range(DEPTH):
        kind, j = i % 3, i // 3
        if kind == 0:
            h, hb = _ssd_layer(h, hb, ssd_in_w[j], ssd_conv_w[j], ssd_conv_b[j], ssd_dt_bias[j],
                               ssd_a_log[j], ssd_d[j], ssd_norm_w[j], ssd_out_w[j], ln_g[i], ln_b[i],
                               f"l{i}_ssd")
        elif kind == 1:
            h, hb = _mla_layer(h, hb, positions, mla_in_w[j], mla_q_norm_w[j], mla_q_up_w[j],
                               mla_kv_norm_w[j], mla_kv_up_w[j], mla_out_w[j], ln_g[i], ln_b[i],
                               f"l{i}_mla")
        else:
            h, hb = _gdn_layer(h, hb, gdn_in_w[j], gdn_conv_w[j], gdn_a_log[j], gdn_dt_bias[j],
                               gdn_norm_w[j], gdn_out_w[j], ln_g[i], ln_b[i], f"l{i}_gdn")
    return h
```

```python
import functools
import math

import jax
import jax.numpy as jnp
from jax import lax
from jax.experimental import pallas as pl
from jax.experimental.pallas import tpu as pltpu

F32 = jnp.float32
BF16 = jnp.bfloat16

D_MODEL = 2048
DEPTH = 4
DEEPNORM_ALPHA = (2.0 * DEPTH) ** 0.25
LN_EPS = 1e-5
RMS_EPS = 1e-6

SSD_D_INNER = 4096
SSD_HEAD_DIM = 64
SSD_N_HEADS = 64
SSD_N_GROUPS = 8
SSD_HEADS_PER_GROUP = SSD_N_HEADS // SSD_N_GROUPS
SSD_D_STATE = 128
SSD_CONV = 4
SSD_CHUNK = 128
SSD_GROUP_CH = SSD_D_INNER // SSD_N_GROUPS
SSD_BC = SSD_N_GROUPS * SSD_D_STATE
SSD_CONV_DIM = SSD_D_INNER + 2 * SSD_BC

MLA_N_HEADS = 16
MLA_Q_RANK = 768
MLA_KV_RANK = 512
MLA_NOPE = 128
MLA_ROPE = 64
MLA_V = 128
MLA_QK = MLA_NOPE + MLA_ROPE
MLA_GATE = MLA_N_HEADS * MLA_V
ROPE_THETA = 10000.0

GDN_N_QK_HEADS = 16
GDN_N_V_HEADS = 32
GDN_DK = 128
GDN_DV = 128
GDN_KEY_DIM = GDN_N_QK_HEADS * GDN_DK
GDN_VAL_DIM = GDN_N_V_HEADS * GDN_DV
GDN_CONV = 4
GDN_CHUNK = 64
GDN_CONV_DIM = 2 * GDN_KEY_DIM + GDN_VAL_DIM

LANES = 128
SUBLANES = 8
LOG2E = math.log2(math.e)
VMEM_LIMIT_BYTES = 56 * 1024 * 1024


def _cparams(sem):
    return pltpu.CompilerParams(dimension_semantics=sem, vmem_limit_bytes=VMEM_LIMIT_BYTES)


def _sigmoid(x):
    return 1.0 / (1.0 + jnp.exp(-x))


def _silu(x):
    return x * _sigmoid(x)


def _softplus(x):
    return jnp.maximum(x, 0.0) + jnp.log1p(jnp.exp(-jnp.abs(x)))


def _dot(a, b):
    return jnp.dot(a, b, preferred_element_type=F32)


def _dot_nt(a, b):
    return lax.dot_general(a, b, (((1,), (1,)), ((), ())), preferred_element_type=F32)


def _dot_tn(a, b):
    return lax.dot_general(a, b, (((0,), (0,)), ((), ())), preferred_element_type=F32)


def _dot_sel(sel_bf16, a):
    hi = a.astype(BF16)
    r1 = a - hi.astype(F32)
    mid = r1.astype(BF16)
    lo = (r1 - mid.astype(F32)).astype(BF16)
    return _dot(sel_bf16, hi) + _dot(sel_bf16, mid) + _dot(sel_bf16, lo)


def _mm_kernel(x_ref, w_ref, o_ref):
    o_ref[...] = _dot(x_ref[...], w_ref[...].astype(BF16)).astype(o_ref.dtype)


def _pick_tile(n, pref):
    t = min(n, pref)
    while n % t:
        t -= LANES
    return t


def _matmul(x, w, name, out_dtype=F32, tm=1024, tn=512):
    T, K = x.shape
    N = w.shape[1]
    tm = min(tm, T)
    tn = _pick_tile(N, tn)
    return pl.pallas_call(
        _mm_kernel,
        grid=(T // tm, N // tn),
        in_specs=[pl.BlockSpec((tm, K), lambda i, j: (i, 0)),
                  pl.BlockSpec((K, tn), lambda i, j: (0, j))],
        out_specs=pl.BlockSpec((tm, tn), lambda i, j: (i, j)),
        out_shape=jax.ShapeDtypeStruct((T, N), out_dtype),
        compiler_params=_cparams(("parallel", "parallel")),
        name=name,
    )(x, w)


def _outproj_ln_kernel(x_ref, w_ref, h_ref, g_ref, b_ref, o_ref, ob_ref):
    k = pl.program_id(1)

    @pl.when(k == 0)
    def _():
        o_ref[...] = jnp.zeros_like(o_ref)

    o_ref[...] += _dot(x_ref[...], w_ref[...].astype(BF16))

    @pl.when(k == pl.num_programs(1) - 1)
    def _():
        r = DEEPNORM_ALPHA * h_ref[...] + o_ref[...]
        mu = jnp.mean(r, axis=-1, keepdims=True)
        d = r - mu
        var = jnp.mean(d * d, axis=-1, keepdims=True)
        out = d * lax.rsqrt(var + LN_EPS) * g_ref[...] + b_ref[...]
        o_ref[...] = out
        ob_ref[...] = out.astype(BF16)


def _outproj_ln(x, w, h, g, b, name, tm=1024, tk=512):
    T, K = x.shape
    N = w.shape[1]
    tm = min(tm, T)
    tk = min(tk, K)
    return pl.pallas_call(
        _outproj_ln_kernel,
        grid=(T // tm, K // tk),
        in_specs=[pl.BlockSpec((tm, tk), lambda i, k: (i, k)),
                  pl.BlockSpec((tk, N), lambda i, k: (k, 0)),
                  pl.BlockSpec((tm, N), lambda i, k: (i, 0)),
                  pl.BlockSpec((1, N), lambda i, k: (0, 0)),
                  pl.BlockSpec((1, N), lambda i, k: (0, 0))],
        out_specs=[pl.BlockSpec((tm, N), lambda i, k: (i, 0)),
                   pl.BlockSpec((tm, N), lambda i, k: (i, 0))],
        out_shape=[jax.ShapeDtypeStruct((T, N), F32), jax.ShapeDtypeStruct((T, N), BF16)],
        compiler_params=_cparams(("parallel", "arbitrary")),
        name=name,
    )(x, w, h, g.reshape(1, N), b.reshape(1, N))


MXU_COLS = 256


def _column_pieces(x_ref, w_ref):
    x = x_ref[...]
    starts = list(range(0, w_ref.shape[1], MXU_COLS))
    nxt = _dot(x, w_ref[:, :MXU_COLS].astype(BF16))
    for p, c in enumerate(starts):
        cur = nxt
        if p + 1 < len(starts):
            nxt = _dot(x, w_ref[:, starts[p + 1]:starts[p + 1] + MXU_COLS].astype(BF16))
        yield c, cur


def _proj_silu_kernel(x_ref, w_ref, o_ref):
    for c, y in _column_pieces(x_ref, w_ref):
        o_ref[:, c:c + MXU_COLS] = _silu(y).astype(o_ref.dtype)


def _proj_conv_kernel(x_ref, w_ref, cw_ref, cb_ref, o_ref, halo_ref, *, taps, tiles_per_seq):
    i, j = pl.program_id(0), pl.program_id(1)
    tm = x_ref.shape[0]

    @pl.when(i % tiles_per_seq == 0)
    def _():
        halo_ref[j] = jnp.zeros(halo_ref.shape[1:], F32)

    for c, y in _column_pieces(x_ref, w_ref):
        cs = slice(c, c + MXU_COLS)
        acc = cb_ref[:, cs] + cw_ref[taps - 1:taps, cs] * y
        for d in range(1, taps):
            acc = acc + cw_ref[taps - 1 - d:taps - d, cs] * pltpu.roll(y, d, 0)
        o_ref[:, cs] = _silu(acc).astype(o_ref.dtype)
        head = jnp.concatenate([halo_ref[j, :, cs], y[:SUBLANES, :]], axis=0)
        halo_ref[j, :, cs] = y[tm - SUBLANES:, :]
        acc = jnp.broadcast_to(cb_ref[:, cs], (SUBLANES, MXU_COLS))
        for k in range(taps):
            off = SUBLANES - (taps - 1) + k
            acc = acc + cw_ref[k:k + 1, cs] * head[off:off + SUBLANES]
        o_ref[:SUBLANES, cs] = _silu(acc).astype(o_ref.dtype)


def _col_tile(c0, n, pref):
    assert c0 % LANES == 0 and n % LANES == 0
    tn = min(n, pref)
    while n % tn or c0 % tn:
        tn -= LANES
    return tn


def _proj_silu(x, w, c0, n, name, out_dtype=F32, tm=1024, tn=1024):
    T, K = x.shape
    tm = min(tm, T)
    tn = _col_tile(c0, n, tn)
    j0 = c0 // tn
    return pl.pallas_call(
        _proj_silu_kernel,
        grid=(T // tm, n // tn),
        in_specs=[pl.BlockSpec((tm, K), lambda i, j: (i, 0)),
                  pl.BlockSpec((K, tn), lambda i, j: (0, j0 + j))],
        out_specs=pl.BlockSpec((tm, tn), lambda i, j: (i, j)),
        out_shape=jax.ShapeDtypeStruct((T, n), out_dtype),
        compiler_params=_cparams(("parallel", "parallel")),
        name=name,
    )(x, w)


def _proj_conv(x, w, c0, n, conv_w, conv_b, cc0, seq_len, name, out_dtype=F32, tm=1024, tn=1024):
    T, K = x.shape
    taps = conv_w.shape[0]
    tm = min(tm, seq_len)
    assert seq_len % tm == 0
    tn = _col_tile(c0, n, tn)
    assert cc0 % tn == 0
    j0, jc0 = c0 // tn, cc0 // tn
    return pl.pallas_call(
        functools.partial(_proj_conv_kernel, taps=taps, tiles_per_seq=seq_len // tm),
        grid=(T // tm, n // tn),
        in_specs=[pl.BlockSpec((tm, K), lambda i, j: (i, 0)),
                  pl.BlockSpec((K, tn), lambda i, j: (0, j0 + j)),
                  pl.BlockSpec((taps, tn), lambda i, j: (0, jc0 + j)),
                  pl.BlockSpec((1, tn), lambda i, j: (0, jc0 + j))],
        out_specs=pl.BlockSpec((tm, tn), lambda i, j: (i, j)),
        out_shape=jax.ShapeDtypeStruct((T, n), out_dtype),
        scratch_shapes=[pltpu.VMEM((n // tn, SUBLANES, tn), F32)],
        compiler_params=_cparams(("arbitrary", "arbitrary")),
        name=name,
    )(x, w, conv_w, conv_b.reshape(1, -1))


def _ssd_kernel(x_ref, b_ref, c_ref, z_ref, dt_ref, hp_ref, d_ref, nw_ref, o_ref,
                state_ref, y_ref):
    L, N, P = SSD_CHUNK, SSD_D_STATE, SSD_HEAD_DIM
    G, GC, HPG = SSD_N_GROUPS, SSD_GROUP_CH, SSD_HEADS_PER_GROUP
    ci = pl.program_id(1)

    @pl.when(ci == 0)
    def _():
        state_ref[...] = jnp.zeros_like(state_ref)

    lane = lax.broadcasted_iota(jnp.int32, (1, LANES), 1)
    rows = lax.broadcasted_iota(jnp.int32, (L, L), 0)
    cols = lax.broadcasted_iota(jnp.int32, (L, L), 1)
    causal = rows >= cols
    tri = causal.astype(BF16)
    lo_half = lane < P

    dt = _softplus(dt_ref[0] + hp_ref[0:1, :])
    neg_a = jnp.where(lane < SSD_N_HEADS, -jnp.exp(hp_ref[1:2, :]), 0.0)
    a_cs = _dot_sel(tri, dt * neg_a) * LOG2E
    a_cs_t = a_cs.T
    dt_t = dt.T
    a_last = a_cs[L - 1:L, :]
    state_w = dt * jnp.exp2(a_last - a_cs)
    chunk_decay = jnp.exp2(a_last)

    def pair_cols(v, h0):
        return jnp.where(lo_half, jnp.broadcast_to(v[:, h0:h0 + 1], (v.shape[0], LANES)),
                         jnp.broadcast_to(v[:, h0 + 1:h0 + 2], (v.shape[0], LANES)))

    cb, y_off, bgt = [], [], []
    for g in range(G):
        bgb = b_ref[0, :, g * N:(g + 1) * N]
        cgb = c_ref[0, :, g * N:(g + 1) * N]
        cb.append(_dot_nt(cgb, bgb))
        y_off.append(_dot(cgb, state_ref[g].astype(BF16)))
        bgt.append(bgb.astype(F32).T.astype(BF16))

    for g in range(G):
        xs = []
        for j in range(HPG // 2):
            h0 = g * HPG + 2 * j
            sl = slice(g * GC + j * LANES, g * GC + (j + 1) * LANES)
            xp = x_ref[0, :, sl]
            xb = xp.astype(BF16)
            zero = jnp.zeros_like(xb)
            m = []
            for h in (h0, h0 + 1):
                col = jnp.broadcast_to(a_cs[:, h:h + 1], (L, L))
                decay = jnp.exp2(jnp.where(causal, col - a_cs_t[h:h + 1, :], -jnp.inf))
                m.append((cb[g] * decay * dt_t[h:h + 1, :]).astype(BF16))
            y = _dot(jnp.concatenate(m, axis=1),
                     jnp.concatenate([jnp.where(lo_half, xb, zero), jnp.where(lo_half, zero, xb)], axis=0))
            y = y + y_off[g][:, j * LANES:(j + 1) * LANES] * jnp.exp2(pair_cols(a_cs, h0))
            y = y + xp * d_ref[:, sl]
            y_ref[:, sl] = y * z_ref[0, :, sl]
            xs.append((xp * pair_cols(state_w, h0)).astype(BF16))
        cd = jnp.concatenate([pair_cols(chunk_decay, g * HPG + 2 * j) for j in range(HPG // 2)], axis=1)
        state_ref[g] = state_ref[g] * cd + _dot(bgt[g], jnp.concatenate(xs, axis=1))

    for g in range(G):
        gsl = slice(g * GC, (g + 1) * GC)
        yg = y_ref[:, gsl]
        ms = jnp.sum(yg * yg, axis=-1, keepdims=True) * (1.0 / GC)
        o_ref[0, :, gsl] = (yg * lax.rsqrt(ms + RMS_EPS) * nw_ref[:, gsl]).astype(o_ref.dtype)


def _ssd_scan(xs, bc, zs, dtp, hp, dskip, normw, name):
    B, S, _ = xs.shape
    L, G, N, DI = SSD_CHUNK, SSD_N_GROUPS, SSD_D_STATE, SSD_D_INNER
    nc = S // L
    return pl.pallas_call(
        _ssd_kernel,
        grid=(B, nc),
        in_specs=[pl.BlockSpec((1, L, DI), lambda b, c: (b, c, 0)),
                  pl.BlockSpec((1, L, SSD_BC), lambda b, c: (b, c, 0)),
                  pl.BlockSpec((1, L, SSD_BC), lambda b, c: (b, c, 1)),
                  pl.BlockSpec((1, L, DI), lambda b, c: (b, c, 0)),
                  pl.BlockSpec((1, L, LANES), lambda b, c: (b, c, 0)),
                  pl.BlockSpec((SUBLANES, LANES), lambda b, c: (0, 0)),
                  pl.BlockSpec((1, DI), lambda b, c: (0, 0)),
                  pl.BlockSpec((1, DI), lambda b, c: (0, 0))],
        out_specs=pl.BlockSpec((1, L, DI), lambda b, c: (b, c, 0)),
        out_shape=jax.ShapeDtypeStruct((B, S, DI), BF16),
        scratch_shapes=[pltpu.VMEM((G, N, SSD_GROUP_CH), F32), pltpu.VMEM((L, DI), F32)],
        compiler_params=_cparams(("parallel", "arbitrary")),
        name=name,
    )(xs, bc, bc, zs, dtp, hp, dskip, normw)


def _pad_heads(a, groups):
    per = a.shape[-1] // groups
    a = a.reshape(a.shape[:-1] + (groups, per))
    a = jnp.pad(a, [(0, 0)] * (a.ndim - 1) + [(0, LANES - per)])
    return a.reshape(a.shape[:-2] + (groups * LANES,))


def _ssd_layer(h, hb, in_w, conv_w, conv_b, dt_bias, a_log, d_skip, norm_w, out_w, ln_g, ln_b, tag):
    B, S, D = h.shape
    T = B * S
    DI = SSD_D_INNER
    o_x, o_bc, o_dt = DI, 2 * DI, DI + SSD_CONV_DIM
    wb = in_w
    w_dt = jnp.pad(wb[:, o_dt:], ((0, 0), (0, LANES - SSD_N_HEADS)))
    zs = _proj_silu(hb, wb, 0, DI, f"{tag}_z").reshape(B, S, DI)
    xs = _proj_conv(hb, wb, o_x, DI, conv_w, conv_b, 0, S, f"{tag}_x").reshape(B, S, DI)
    bc = _proj_conv(hb, wb, o_bc, 2 * SSD_BC, conv_w, conv_b, DI, S, f"{tag}_bc",
                    out_dtype=BF16).reshape(B, S, 2 * SSD_BC)
    dtp = _matmul(hb, w_dt, f"{tag}_dt").reshape(B, S, LANES)
    hp = jnp.zeros((SUBLANES, LANES), F32)
    hp = hp.at[0, :SSD_N_HEADS].set(dt_bias)
    hp = hp.at[1, :SSD_N_HEADS].set(a_log)
    dskip = jnp.repeat(d_skip, SSD_HEAD_DIM).reshape(1, DI)
    y = _ssd_scan(xs, bc, zs, dtp, hp, dskip, norm_w.reshape(1, DI), f"{tag}_scan")
    hn, hnb = _outproj_ln(y.reshape(T, SSD_D_INNER), out_w.astype(BF16), h.reshape(T, D),
                          ln_g, ln_b, f"{tag}_out")
    return hn.reshape(B, S, D), hnb


MLA_Q_SCALE = MLA_QK ** -0.5 * math.log2(math.e)


def _rope_table(pos_ref, invf_ref):
    lane = lax.broadcasted_iota(jnp.int32, (1, LANES), 1)
    ang = pos_ref[0].astype(F32) * invf_ref[...]
    q = MLA_ROPE // 2
    return jnp.where(lane < 2 * q, jnp.cos(ang), jnp.where(lane < 3 * q, -1.0, 1.0) * jnp.sin(ang))


def _rope_apply(xr, table):
    prod = xr * table
    return prod + pltpu.roll(prod, MLA_ROPE, 1)


def _rms_bf16(x, w):
    ms = jnp.mean(x * x, axis=-1, keepdims=True)
    return (x * lax.rsqrt(ms + RMS_EPS) * w).astype(BF16)


def _mla_q_kernel(qc_ref, nw_ref, w_ref, pos_ref, invf_ref, o_ref):
    xn = _rms_bf16(qc_ref[0], nw_ref[...])
    table = _rope_table(pos_ref, invf_ref)
    hw = MLA_NOPE + 2 * MLA_ROPE
    for hd in range(MLA_N_HEADS):
        r = _dot(xn, w_ref[:, hd * hw:(hd + 1) * hw])
        roped = _rope_apply(r[:, MLA_NOPE:], table)
        o_ref[0, hd, :, :MLA_NOPE] = (r[:, :MLA_NOPE] * MLA_Q_SCALE).astype(BF16)
        o_ref[0, hd, :, MLA_NOPE:] = (roped[:, :MLA_ROPE] * MLA_Q_SCALE).astype(BF16)


def _mla_kv_kernel(kvc_ref, nw_ref, w_ref, kr_ref, pos_ref, invf_ref, k_ref, v_ref):
    xn = _rms_bf16(kvc_ref[0], nw_ref[...])
    k_rope = _rope_apply(kr_ref[0], _rope_table(pos_ref, invf_ref))[:, :MLA_ROPE].astype(BF16)
    hw = MLA_NOPE + MLA_V
    for hd in range(MLA_N_HEADS):
        r = _dot(xn, w_ref[:, hd * hw:(hd + 1) * hw])
        k_ref[0, hd, :, :MLA_NOPE] = r[:, :MLA_NOPE].astype(BF16)
        k_ref[0, hd, :, MLA_NOPE:] = k_rope
        v_ref[0, hd] = r[:, MLA_NOPE:].astype(BF16)


def _mla_attn_kernel(q_ref, k_ref, v_ref, z_ref, o_ref, *, tk):
    qi = pl.program_id(2)
    halves = (q_ref[0, 0, :tk, :], q_ref[0, 0, tk:, :])
    rows = lax.broadcasted_iota(jnp.int32, (tk, tk), 0)
    cols = lax.broadcasted_iota(jnp.int32, (tk, tk), 1)

    def block(kb, carry, masked):
        start = pl.multiple_of(kb * tk, tk)
        k = k_ref[0, 0, pl.ds(start, tk), :]
        v = v_ref[0, 0, pl.ds(start, tk), :]
        live = [e for e in range(2) if masked[e] is not None]
        s = {e: _dot_nt(halves[e], k) for e in live}
        out = list(carry)
        for e in live:
            m, l, acc = carry[e]
            se = jnp.where(rows >= cols, s[e], -jnp.inf) if masked[e] else s[e]
            m_new = jnp.maximum(m, jnp.max(se, axis=-1, keepdims=True))
            p = jnp.exp2(se - m_new)
            alpha = jnp.exp2(m - m_new)
            l = alpha * l + jnp.sum(p, axis=-1, keepdims=True)
            out[e] = (m_new, l, alpha * acc + _dot(p.astype(BF16), v))
        return tuple(out)

    init = tuple((jnp.full((tk, 1), -jnp.inf, F32), jnp.zeros((tk, 1), F32), jnp.zeros((tk, MLA_V), F32))
                 for _ in range(2))
    carry = lax.fori_loop(0, 2 * qi, lambda kb, c: block(kb, c, (False, False)), init)
    carry = block(2 * qi, carry, (True, False))
    carry = block(2 * qi + 1, carry, (None, True))
    for e in range(2):
        _, l, acc = carry[e]
        rs = slice(e * tk, (e + 1) * tk)
        o_ref[0, rs, :] = (acc / l * z_ref[0, rs, :]).astype(o_ref.dtype)


def _mla_layer(h, hb, positions, in_w, q_norm_w, q_up_w, kv_norm_w, kv_up_w, out_w, ln_g, ln_b, tag):
    B, S, D = h.shape
    T = B * S
    H = MLA_N_HEADS
    o_kv = MLA_Q_RANK
    o_kr = o_kv + MLA_KV_RANK
    o_z = o_kr + MLA_ROPE
    half = MLA_ROPE // 2
    wb = in_w
    w_kr = wb[:, o_kr:o_z]
    w_kr2 = jnp.concatenate([w_kr, w_kr[:, half:], w_kr[:, :half]], axis=1)
    q_c = _matmul(hb, wb[:, :o_kv], f"{tag}_qc", tn=768).reshape(B, S, MLA_Q_RANK)
    kv_c = _matmul(hb, wb[:, o_kv:o_kr], f"{tag}_kvc").reshape(B, S, MLA_KV_RANK)
    kr = _matmul(hb, w_kr2, f"{tag}_kr").reshape(B, S, LANES)
    z = _proj_silu(hb, wb[:, o_z:], 0, MLA_GATE, f"{tag}_z").reshape(B, S, MLA_GATE)

    wq = q_up_w.reshape(MLA_Q_RANK, H, MLA_QK)
    wq_r = wq[:, :, MLA_NOPE:]
    wq3 = jnp.concatenate([wq, wq_r[:, :, half:], wq_r[:, :, :half]], axis=2)
    wq3 = wq3.reshape(MLA_Q_RANK, H * (MLA_QK + MLA_ROPE)).astype(BF16)
    wkv = kv_up_w.astype(BF16)

    inv_freq = ROPE_THETA ** (-jnp.arange(0, MLA_ROPE, 2, dtype=F32) / MLA_ROPE)
    invf = jnp.tile(inv_freq, LANES // half).reshape(1, LANES)
    pos3 = positions.reshape(B, S, 1)

    tm = min(512, S)
    qf = pl.pallas_call(
        _mla_q_kernel,
        grid=(B, S // tm),
        in_specs=[pl.BlockSpec((1, tm, MLA_Q_RANK), lambda b, i: (b, i, 0)),
                  pl.BlockSpec((1, MLA_Q_RANK), lambda b, i: (0, 0)),
                  pl.BlockSpec(wq3.shape, lambda b, i: (0, 0)),
                  pl.BlockSpec((1, tm, 1), lambda b, i: (b, i, 0)),
                  pl.BlockSpec((1, LANES), lambda b, i: (0, 0))],
        out_specs=pl.BlockSpec((1, H, tm, MLA_QK), lambda b, i: (b, 0, i, 0)),
        out_shape=jax.ShapeDtypeStruct((B, H, S, MLA_QK), BF16),
        compiler_params=_cparams(("parallel", "parallel")),
        name=f"{tag}_qup",
    )(q_c, q_norm_w.reshape(1, -1), wq3, pos3, invf)

    kf, vf = pl.pallas_call(
        _mla_kv_kernel,
        grid=(B, S // tm),
        in_specs=[pl.BlockSpec((1, tm, MLA_KV_RANK), lambda b, i: (b, i, 0)),
                  pl.BlockSpec((1, MLA_KV_RANK), lambda b, i: (0, 0)),
                  pl.BlockSpec(wkv.shape, lambda b, i: (0, 0)),
                  pl.BlockSpec((1, tm, LANES), lambda b, i: (b, i, 0)),
                  pl.BlockSpec((1, tm, 1), lambda b, i: (b, i, 0)),
                  pl.BlockSpec((1, LANES), lambda b, i: (0, 0))],
        out_specs=[pl.BlockSpec((1, H, tm, MLA_QK), lambda b, i: (b, 0, i, 0)),
                   pl.BlockSpec((1, H, tm, MLA_V), lambda b, i: (b, 0, i, 0))],
        out_shape=[jax.ShapeDtypeStruct((B, H, S, MLA_QK), BF16),
                   jax.ShapeDtypeStruct((B, H, S, MLA_V), BF16)],
        compiler_params=_cparams(("parallel", "parallel")),
        name=f"{tag}_kvup",
    )(kv_c, kv_norm_w.reshape(1, -1), wkv, kr, pos3, invf)

    tq = min(1024, S)
    o = pl.pallas_call(
        functools.partial(_mla_attn_kernel, tk=tq // 2),
        grid=(B, H, S // tq),
        in_specs=[pl.BlockSpec((1, 1, tq, MLA_QK), lambda b, hd, i: (b, hd, i, 0)),
                  pl.BlockSpec((1, 1, S, MLA_QK), lambda b, hd, i: (b, hd, 0, 0)),
                  pl.BlockSpec((1, 1, S, MLA_V), lambda b, hd, i: (b, hd, 0, 0)),
                  pl.BlockSpec((1, tq, MLA_V), lambda b, hd, i: (b, i, hd))],
        out_specs=pl.BlockSpec((1, tq, MLA_V), lambda b, hd, i: (b, i, hd)),
        out_shape=jax.ShapeDtypeStruct((B, S, MLA_GATE), BF16),
        compiler_params=_cparams(("parallel", "parallel", "arbitrary")),
        name=f"{tag}_attn",
    )(qf, kf, vf, z)

    hn, hnb = _outproj_ln(o.reshape(T, MLA_GATE), out_w.astype(BF16), h.reshape(T, D),
                          ln_g, ln_b, f"{tag}_out")
    return hn.reshape(B, S, D), hnb


GDN_QK_PER_STEP = 4
GDN_V_PER_STEP = 2 * GDN_QK_PER_STEP
GDN_CHUNKS_PER_STEP = 8
GDN_ROWS = GDN_CHUNKS_PER_STEP * GDN_CHUNK


def _block_mask(nrows, ncols, rblk, cblk):
    r = lax.broadcasted_iota(jnp.int32, (nrows, ncols), 0) // rblk
    c = lax.broadcasted_iota(jnp.int32, (nrows, ncols), 1) // cblk
    return jnp.where(r == c, 1.0, 0.0).astype(BF16)


def _block_diag(x, reps, mask):
    return jnp.concatenate([x] * reps, axis=0) * mask


def _place_blocks(x, layout):
    zero = jnp.zeros((x.shape[0], LANES), x.dtype)
    rows = [jnp.concatenate([zero if b is None else x[:, b * LANES:(b + 1) * LANES] for b in row], axis=1)
            for row in layout]
    return jnp.concatenate(rows, axis=0)


def _gdn_kernel(q_ref, k_ref, v_ref, z_ref, b_ref, a_ref, hp_ref, nw_ref, o_ref,
                state_ref, u_s, wq_s, kd_s, qkm_s, egl_s, qn_s, kn_s):
    L, R, C = GDN_CHUNK, GDN_ROWS, GDN_CHUNKS_PER_STEP
    NQ, NV = GDN_QK_PER_STEP, GDN_V_PER_STEP
    W = NV * L
    ri = pl.program_id(2)

    @pl.when(ri == 0)
    def _():
        state_ref[...] = jnp.zeros_like(state_ref)

    beta = _sigmoid(b_ref[0])
    g = -jnp.exp(hp_ref[0, 0:1, :]) * _softplus(a_ref[0] + hp_ref[0, 1:2, :])
    rr = lax.broadcasted_iota(jnp.int32, (R, R), 0)
    cc = lax.broadcasted_iota(jnp.int32, (R, R), 1)
    blocktri = ((rr >= cc) & (rr // L == cc // L)).astype(BF16)
    gcs = _dot_sel(blocktri, g) * LOG2E
    gcs_t = gcs.T
    beta_t = beta.T

    lane = lax.broadcasted_iota(jnp.int32, (1, LANES), 1)
    lo = lane < L
    rows_w = lax.broadcasted_iota(jnp.int32, (L, W), 0)
    s_w = lax.broadcasted_iota(jnp.int32, (L, W), 1) % L
    incl_w = rows_w >= s_w
    strict_w = rows_w > s_w
    eye_w = jnp.where(rows_w == s_w, 1.0, 0.0)
    head_sel = (lax.broadcasted_iota(jnp.int32, (NV, W), 0)
                == lax.broadcasted_iota(jnp.int32, (NV, W), 1) // L)
    m_p = _block_mask(4 * L, 4 * L, L, L)
    kk_layout = [[hv // 2 if cb == hv // 2 else None for cb in range(NQ)] for hv in range(NV)]
    uw_layout = [[0, None, 2, None], [None, 1, None, 3]]
    pair_layout = [[0, None], [None, 1]]
    nw = nw_ref[...]

    def row_vec(t_heads, c):
        x = t_heads[0:NV, c * L:(c + 1) * L]
        x8 = jnp.concatenate([x] * NV, axis=1)
        return jnp.sum(jnp.where(head_sel, x8, 0.0), axis=0, keepdims=True)

    def col_wide(cols):
        return jnp.concatenate([jnp.where(lo, cols[2 * j], cols[2 * j + 1]) for j in range(NV // 2)], axis=1)

    chunks = range(C)
    rows_of = [slice(c * L, (c + 1) * L) for c in chunks]
    kq = []
    for c in chunks:
        qn, kn = [], []
        for i in range(NQ):
            q = q_ref[0, rows_of[c], i * GDN_DK:(i + 1) * GDN_DK]
            k = k_ref[0, rows_of[c], i * GDN_DK:(i + 1) * GDN_DK]
            qn.append(q * lax.rsqrt(jnp.sum(q * q, axis=-1, keepdims=True) + RMS_EPS) * (GDN_DK ** -0.5))
            kn.append(k * lax.rsqrt(jnp.sum(k * k, axis=-1, keepdims=True) + RMS_EPS))
        qn_s[c] = jnp.concatenate(qn, axis=1)
        kn_s[c] = jnp.concatenate(kn, axis=1)
        qb4 = jnp.concatenate([x.astype(BF16) for x in qn], axis=1)
        kb4 = jnp.concatenate([x.astype(BF16) for x in kn], axis=1)
        kq.append(_dot_nt(jnp.concatenate([qb4, kb4], axis=0), _place_blocks(kb4, kk_layout)))

    p_cur, t_cur = {}, {}
    for c in chunks:
        sl = rows_of[c]
        colg = [jnp.broadcast_to(gcs[sl, h:h + 1], (L, LANES)) for h in range(NV)]
        colb = [jnp.broadcast_to(beta[sl, h:h + 1], (L, LANES)) for h in range(NV)]
        decay = jnp.exp2(jnp.where(incl_w, col_wide(colg) - row_vec(gcs_t, c), -jnp.inf))
        a_w = jnp.where(strict_w, kq[c][L:] * col_wide(colb) * decay, 0.0)
        qkm_s[c] = (kq[c][:L] * decay).astype(BF16)
        for gi in range(NV // 4):
            gs = slice(gi * 4 * L, (gi + 1) * 4 * L)
            t_cur[c, gi] = eye_w[:, gs] - a_w[:, gs]
            pb = a_w[:, gs].astype(BF16)
            p_cur[c, gi] = _dot(pb, _block_diag(pb, 4, m_p))

    for it in range(5):
        for key in sorted(p_cur):
            pb = p_cur[key].astype(BF16)
            bd = _block_diag(pb, 4, m_p)
            t = t_cur[key]
            if it < 4:
                rr2 = _dot(jnp.concatenate([pb, t.astype(BF16)], axis=0), bd)
                p_cur[key] = rr2[:L]
                t_cur[key] = t + rr2[L:]
            else:
                t_cur[key] = t + _dot(t.astype(BF16), bd)

    for c in chunks:
        sl = rows_of[c]
        t_w = jnp.concatenate([t_cur[c, gi] for gi in range(NV // 4)], axis=1)
        tb_w = (t_w * row_vec(beta_t, c)).astype(BF16)
        colg = [jnp.broadcast_to(gcs[sl, h:h + 1], (L, LANES)) for h in range(NV)]
        eg = [jnp.exp2(x) for x in colg]
        gl = [gcs[c * L + L - 1:c * L + L, h:h + 1] for h in range(NV)]
        for i in range(NQ):
            h0, h1 = 2 * i, 2 * i + 1
            psl = slice(i * 2 * GDN_DV, (i + 1) * 2 * GDN_DV)
            qni = qn_s[c, :, i * GDN_DK:(i + 1) * GDN_DK]
            kni = kn_s[c, :, i * GDN_DK:(i + 1) * GDN_DK]
            base = jnp.concatenate([v_ref[0, sl, psl].astype(BF16),
                                    (kni * eg[h0]).astype(BF16), (kni * eg[h1]).astype(BF16)], axis=1)
            uw = _dot(tb_w[:, i * 2 * L:(i + 1) * 2 * L], _place_blocks(base, uw_layout))
            u_s[c, :, psl] = uw[:, :2 * GDN_DV]
            qd = jnp.concatenate([(qni * eg[h0]).astype(BF16), (qni * eg[h1]).astype(BF16)], axis=1)
            wq_s[c, i] = jnp.concatenate([uw[:, 2 * GDN_DV:].astype(BF16), qd], axis=0)
            kd_s[c, i] = jnp.concatenate([(kni * jnp.exp2(gl[h0] - colg[h0])).astype(BF16),
                                          (kni * jnp.exp2(gl[h1] - colg[h1])).astype(BF16)], axis=0)
            egl_s[c, i] = jnp.concatenate([jnp.broadcast_to(jnp.exp2(gl[h0]), (SUBLANES, GDN_DV)),
                                           jnp.broadcast_to(jnp.exp2(gl[h1]), (SUBLANES, GDN_DV))], axis=1)

    pairs = range(NQ)
    for c in chunks:
        sl = rows_of[c]
        st = [state_ref[i] for i in pairs]
        r1 = [_dot(wq_s[c, i], _place_blocks(st[i].astype(BF16), pair_layout)) for i in pairs]
        bdv = []
        for i in pairs:
            psl = slice(i * 2 * GDN_DV, (i + 1) * 2 * GDN_DV)
            vb = (u_s[c, :, psl] - r1[i][:L]).astype(BF16)
            bdv.append(_place_blocks(vb, pair_layout))
        for i in pairs:
            state_ref[i] = st[i] * egl_s[c, i, 0:1, :] + _dot_tn(kd_s[c, i], bdv[i])
        for i in pairs:
            o = r1[i][L:] + _dot(qkm_s[c, :, i * 2 * L:(i + 1) * 2 * L], bdv[i])
            for e in range(2):
                oh = o[:, e * GDN_DV:(e + 1) * GDN_DV]
                on = oh * lax.rsqrt(jnp.mean(oh * oh, axis=-1, keepdims=True) + RMS_EPS) * nw
                hsl = slice((2 * i + e) * GDN_DV, (2 * i + e + 1) * GDN_DV)
                o_ref[0, sl, hsl] = (on * z_ref[0, sl, hsl]).astype(o_ref.dtype)


def _gdn_core(qk, v, zs, bp, ap, hp, normw, name):
    B, S, _ = qk.shape
    R = min(GDN_ROWS, S)
    assert R == GDN_ROWS
    HG = GDN_N_QK_HEADS // GDN_QK_PER_STEP
    wq = GDN_QK_PER_STEP * GDN_DK
    wv = GDN_V_PER_STEP * GDN_DV
    k_blk0 = GDN_KEY_DIM // wq
    return pl.pallas_call(
        _gdn_kernel,
        grid=(B, HG, S // R),
        in_specs=[pl.BlockSpec((1, R, wq), lambda b, g, r: (b, r, g)),
                  pl.BlockSpec((1, R, wq), lambda b, g, r: (b, r, k_blk0 + g)),
                  pl.BlockSpec((1, R, wv), lambda b, g, r: (b, r, g)),
                  pl.BlockSpec((1, R, wv), lambda b, g, r: (b, r, g)),
                  pl.BlockSpec((1, R, LANES), lambda b, g, r: (b, r, g)),
                  pl.BlockSpec((1, R, LANES), lambda b, g, r: (b, r, g)),
                  pl.BlockSpec((1, SUBLANES, LANES), lambda b, g, r: (g, 0, 0)),
                  pl.BlockSpec((1, GDN_DV), lambda b, g, r: (0, 0))],
        out_specs=pl.BlockSpec((1, R, wv), lambda b, g, r: (b, r, g)),
        out_shape=jax.ShapeDtypeStruct((B, S, GDN_VAL_DIM), BF16),
        scratch_shapes=[pltpu.VMEM((GDN_QK_PER_STEP, GDN_DK, 2 * GDN_DV), F32),
                        pltpu.VMEM((GDN_CHUNKS_PER_STEP, GDN_CHUNK, wv), F32),
                        pltpu.VMEM((GDN_CHUNKS_PER_STEP, GDN_QK_PER_STEP, 2 * GDN_CHUNK, 2 * GDN_DV), BF16),
                        pltpu.VMEM((GDN_CHUNKS_PER_STEP, GDN_QK_PER_STEP, 2 * GDN_CHUNK, GDN_DK), BF16),
                        pltpu.VMEM((GDN_CHUNKS_PER_STEP, GDN_CHUNK, GDN_V_PER_STEP * GDN_CHUNK), BF16),
                        pltpu.VMEM((GDN_CHUNKS_PER_STEP, GDN_QK_PER_STEP, SUBLANES, 2 * GDN_DV), F32),
                        pltpu.VMEM((GDN_CHUNKS_PER_STEP, GDN_CHUNK, wq), F32),
                        pltpu.VMEM((GDN_CHUNKS_PER_STEP, GDN_CHUNK, wq), F32)],
        compiler_params=_cparams(("parallel", "parallel", "arbitrary")),
        name=name,
    )(qk, qk, v, zs, bp, ap, hp, normw)


def _gdn_layer(h, hb, in_w, conv_w, a_log, dt_bias, norm_w, out_w, ln_g, ln_b, tag):
    B, S, D = h.shape
    T = B * S
    o_v, o_z, o_ba = 2 * GDN_KEY_DIM, GDN_CONV_DIM, GDN_CONV_DIM + GDN_VAL_DIM
    HG = GDN_N_QK_HEADS // GDN_QK_PER_STEP
    no_bias = jnp.zeros((GDN_CONV_DIM,), F32)
    wb = in_w
    qk = _proj_conv(hb, wb, 0, o_v, conv_w, no_bias, 0, S, f"{tag}_qk").reshape(B, S, o_v)
    v = _proj_conv(hb, wb, o_v, GDN_VAL_DIM, conv_w, no_bias, o_v, S, f"{tag}_v",
                   out_dtype=BF16).reshape(B, S, GDN_VAL_DIM)
    zs = _proj_silu(hb, wb, o_z, GDN_VAL_DIM, f"{tag}_z").reshape(B, S, GDN_VAL_DIM)
    ba = _matmul(hb, wb[:, o_ba:], f"{tag}_ba").reshape(B, S, 2 * GDN_N_V_HEADS)
    bp = _pad_heads(ba[..., :GDN_N_V_HEADS], HG)
    ap = _pad_heads(ba[..., GDN_N_V_HEADS:], HG)
    hp = jnp.zeros((HG, SUBLANES, LANES), F32)
    hp = hp.at[:, 0, :GDN_V_PER_STEP].set(a_log.reshape(HG, -1))
    hp = hp.at[:, 1, :GDN_V_PER_STEP].set(dt_bias.reshape(HG, -1))
    o = _gdn_core(qk, v, zs, bp, ap, hp, norm_w.reshape(1, GDN_DV), f"{tag}_core")
    hn, hnb = _outproj_ln(o.reshape(T, GDN_VAL_DIM), out_w.astype(BF16), h.reshape(T, D),
                          ln_g, ln_b, f"{tag}_out")
    return hn.reshape(B, S, D), hnb


def kernel(x, positions, ssd_in_w, ssd_conv_w, ssd_conv_b, ssd_dt_bias, ssd_a_log, ssd_d, ssd_norm_w,
           ssd_out_w, mla_in_w, mla_q_norm_w, mla_q_up_w, mla_kv_norm_w, mla_kv_up_w, mla_out_w,
           gdn_in_w, gdn_conv_w, gdn_a_log, gdn_dt_bias, gdn_norm_w, gdn_out_w, ln_g, ln_b):
    B, S, D = x.shape
    h = x
    hb = x.reshape(B * S, D).astype(BF16)
    for i in range(DEPTH):
        kind, j = i % 3, i // 3
        if kind == 0:
            h, hb = _ssd_layer(h, hb, ssd_in_w[j], ssd_conv_w[j], ssd_conv_b[j], ssd_dt_bias[j],
                               ssd_a_log[j], ssd_d[j], ssd_norm_w[j], ssd_out_w[j], ln_g[i], ln_b[i],
                               f"l{i}_ssd")
        elif kind == 1:
            h, hb = _mla_layer(h, hb, positions, mla_in_w[j], mla_q_norm_w[j], mla_q_up_w[j],
                               mla_kv_norm_w[j], mla_kv_up_w[j], mla_out_w[j], ln_g[i], ln_b[i],
                               f"l{i}_mla")
        else:
            h, hb = _gdn_layer(h, hb, gdn_in_w[j], gdn_conv_w[j], gdn_a_log[j], gdn_dt_bias[j],
                               gdn_norm_w[j], gdn_out_w[j], ln_g[i], ln_b[i], f"l{i}_gdn")
    return h
```

```python
import functools
import math

import jax
import jax.numpy as jnp
from jax import lax
from jax.experimental import pallas as pl
from jax.experimental.pallas import tpu as pltpu

F32 = jnp.float32
BF16 = jnp.bfloat16

D_MODEL = 2048
DEPTH = 4
DEEPNORM_ALPHA = (2.0 * DEPTH) ** 0.25
LN_EPS = 1e-5
RMS_EPS = 1e-6

SSD_D_INNER = 4096
SSD_HEAD_DIM = 64
SSD_N_HEADS = 64
SSD_N_GROUPS = 8
SSD_HEADS_PER_GROUP = SSD_N_HEADS // SSD_N_GROUPS
SSD_D_STATE = 128
SSD_CONV = 4
SSD_CHUNK = 128
SSD_GROUP_CH = SSD_D_INNER // SSD_N_GROUPS
SSD_BC = SSD_N_GROUPS * SSD_D_STATE
SSD_CONV_DIM = SSD_D_INNER + 2 * SSD_BC

MLA_N_HEADS = 16
MLA_Q_RANK = 768
MLA_KV_RANK = 512
MLA_NOPE = 128
MLA_ROPE = 64
MLA_V = 128
MLA_QK = MLA_NOPE + MLA_ROPE
MLA_GATE = MLA_N_HEADS * MLA_V
ROPE_THETA = 10000.0

GDN_N_QK_HEADS = 16
GDN_N_V_HEADS = 32
GDN_DK = 128
GDN_DV = 128
GDN_KEY_DIM = GDN_N_QK_HEADS * GDN_DK
GDN_VAL_DIM = GDN_N_V_HEADS * GDN_DV
GDN_CONV = 4
GDN_CHUNK = 64
GDN_CONV_DIM = 2 * GDN_KEY_DIM + GDN_VAL_DIM

LANES = 128
SUBLANES = 8
LOG2E = math.log2(math.e)
VMEM_LIMIT_BYTES = 56 * 1024 * 1024


def _cparams(sem):
    return pltpu.CompilerParams(dimension_semantics=sem, vmem_limit_bytes=VMEM_LIMIT_BYTES)


def _sigmoid(x):
    return 1.0 / (1.0 + jnp.exp(-x))


def _silu(x):
    return x * _sigmoid(x)


def _softplus(x):
    return jnp.maximum(x, 0.0) + jnp.log1p(jnp.exp(-jnp.abs(x)))


def _dot(a, b):
    return jnp.dot(a, b, preferred_element_type=F32)


def _dot_nt(a, b):
    return lax.dot_general(a, b, (((1,), (1,)), ((), ())), preferred_element_type=F32)


def _dot_tn(a, b):
    return lax.dot_general(a, b, (((0,), (0,)), ((), ())), preferred_element_type=F32)


def _dot_sel(sel_bf16, a):
    hi = a.astype(BF16)
    r1 = a - hi.astype(F32)
    mid = r1.astype(BF16)
    lo = (r1 - mid.astype(F32)).astype(BF16)
    return _dot(sel_bf16, hi) + _dot(sel_bf16, mid) + _dot(sel_bf16, lo)


def _mm_kernel(x_ref, w_ref, o_ref):
    o_ref[...] = _dot(x_ref[...], w_ref[...].astype(BF16)).astype(o_ref.dtype)


def _pick_tile(n, pref):
    t = min(n, pref)
    while n % t:
        t -= LANES
    return t


def _matmul(x, w, name, out_dtype=F32, tm=1024, tn=512):
    T, K = x.shape
    N = w.shape[1]
    tm = min(tm, T)
    tn = _pick_tile(N, tn)
    return pl.pallas_call(
        _mm_kernel,
        grid=(T // tm, N // tn),
        in_specs=[pl.BlockSpec((tm, K), lambda i, j: (i, 0)),
                  pl.BlockSpec((K, tn), lambda i, j: (0, j))],
        out_specs=pl.BlockSpec((tm, tn), lambda i, j: (i, j)),
        out_shape=jax.ShapeDtypeStruct((T, N), out_dtype),
        compiler_params=_cparams(("parallel", "parallel")),
        name=name,
    )(x, w)


def _outproj_ln_kernel(x_ref, w_ref, h_ref, g_ref, b_ref, o_ref, ob_ref):
    k = pl.program_id(1)

    @pl.when(k == 0)
    def _():
        o_ref[...] = jnp.zeros_like(o_ref)

    o_ref[...] += _dot(x_ref[...], w_ref[...].astype(BF16))

    @pl.when(k == pl.num_programs(1) - 1)
    def _():
        r = DEEPNORM_ALPHA * h_ref[...] + o_ref[...]
        mu = jnp.mean(r, axis=-1, keepdims=True)
        d = r - mu
        var = jnp.mean(d * d, axis=-1, keepdims=True)
        out = d * lax.rsqrt(var + LN_EPS) * g_ref[...] + b_ref[...]
        o_ref[...] = out
        ob_ref[...] = out.astype(BF16)


def _outproj_ln(x, w, h, g, b, name, tm=1024, tk=512):
    T, K = x.shape
    N = w.shape[1]
    tm = min(tm, T)
    tk = min(tk, K)
    return pl.pallas_call(
        _outproj_ln_kernel,
        grid=(T // tm, K // tk),
        in_specs=[pl.BlockSpec((tm, tk), lambda i, k: (i, k)),
                  pl.BlockSpec((tk, N), lambda i, k: (k, 0)),
                  pl.BlockSpec((tm, N), lambda i, k: (i, 0)),
                  pl.BlockSpec((1, N), lambda i, k: (0, 0)),
                  pl.BlockSpec((1, N), lambda i, k: (0, 0))],
        out_specs=[pl.BlockSpec((tm, N), lambda i, k: (i, 0)),
                   pl.BlockSpec((tm, N), lambda i, k: (i, 0))],
        out_shape=[jax.ShapeDtypeStruct((T, N), F32), jax.ShapeDtypeStruct((T, N), BF16)],
        compiler_params=_cparams(("parallel", "arbitrary")),
        name=name,
    )(x, w, h, g.reshape(1, N), b.reshape(1, N))


MXU_COLS = 256


def _column_pieces(x_ref, w_ref):
    x = x_ref[...]
    starts = list(range(0, w_ref.shape[2], MXU_COLS))
    nxt = _dot(x, w_ref[0, :, :MXU_COLS].astype(BF16))
    for p, c in enumerate(starts):
        cur = nxt
        if p + 1 < len(starts):
            nxt = _dot(x, w_ref[0, :, starts[p + 1]:starts[p + 1] + MXU_COLS].astype(BF16))
        yield c, cur


def _proj_silu_kernel(x_ref, w_ref, o_ref):
    for c, y in _column_pieces(x_ref, w_ref):
        o_ref[:, c:c + MXU_COLS] = _silu(y).astype(o_ref.dtype)


def _proj_conv_kernel(x_ref, w_ref, cw_ref, cb_ref, o_ref, halo_ref, *, taps, tiles_per_seq):
    i, j = pl.program_id(0), pl.program_id(1)
    tm = x_ref.shape[0]

    @pl.when(i % tiles_per_seq == 0)
    def _():
        halo_ref[j] = jnp.zeros(halo_ref.shape[1:], F32)

    for c, y in _column_pieces(x_ref, w_ref):
        cs = slice(c, c + MXU_COLS)
        acc = cb_ref[:, cs] + cw_ref[taps - 1:taps, cs] * y
        for d in range(1, taps):
            acc = acc + cw_ref[taps - 1 - d:taps - d, cs] * pltpu.roll(y, d, 0)
        o_ref[:, cs] = _silu(acc).astype(o_ref.dtype)
        head = jnp.concatenate([halo_ref[j, :, cs], y[:SUBLANES, :]], axis=0)
        halo_ref[j, :, cs] = y[tm - SUBLANES:, :]
        acc = jnp.broadcast_to(cb_ref[:, cs], (SUBLANES, MXU_COLS))
        for k in range(taps):
            off = SUBLANES - (taps - 1) + k
            acc = acc + cw_ref[k:k + 1, cs] * head[off:off + SUBLANES]
        o_ref[:SUBLANES, cs] = _silu(acc).astype(o_ref.dtype)


def _col_tile(c0, n, pref):
    assert c0 % LANES == 0 and n % LANES == 0
    tn = min(n, pref)
    while n % tn or c0 % tn:
        tn -= LANES
    return tn


def _proj_silu(x, w, layer, c0, n, name, out_dtype=F32, tm=1024, tn=1024):
    T, K = x.shape
    tm = min(tm, T)
    tn = _col_tile(c0, n, tn)
    j0 = c0 // tn
    return pl.pallas_call(
        _proj_silu_kernel,
        grid=(T // tm, n // tn),
        in_specs=[pl.BlockSpec((tm, K), lambda i, j: (i, 0)),
                  pl.BlockSpec((1, K, tn), lambda i, j: (layer, 0, j0 + j))],
        out_specs=pl.BlockSpec((tm, tn), lambda i, j: (i, j)),
        out_shape=jax.ShapeDtypeStruct((T, n), out_dtype),
        compiler_params=_cparams(("parallel", "parallel")),
        name=name,
    )(x, w)


def _proj_conv(x, w, layer, c0, n, conv_w, conv_b, cc0, seq_len, name, out_dtype=F32, tm=1024, tn=1024):
    T, K = x.shape
    taps = conv_w.shape[0]
    tm = min(tm, seq_len)
    assert seq_len % tm == 0
    tn = _col_tile(c0, n, tn)
    assert cc0 % tn == 0
    j0, jc0 = c0 // tn, cc0 // tn
    return pl.pallas_call(
        functools.partial(_proj_conv_kernel, taps=taps, tiles_per_seq=seq_len // tm),
        grid=(T // tm, n // tn),
        in_specs=[pl.BlockSpec((tm, K), lambda i, j: (i, 0)),
                  pl.BlockSpec((1, K, tn), lambda i, j: (layer, 0, j0 + j)),
                  pl.BlockSpec((taps, tn), lambda i, j: (0, jc0 + j)),
                  pl.BlockSpec((1, tn), lambda i, j: (0, jc0 + j))],
        out_specs=pl.BlockSpec((tm, tn), lambda i, j: (i, j)),
        out_shape=jax.ShapeDtypeStruct((T, n), out_dtype),
        scratch_shapes=[pltpu.VMEM((n // tn, SUBLANES, tn), F32)],
        compiler_params=_cparams(("arbitrary", "arbitrary")),
        name=name,
    )(x, w, conv_w, conv_b.reshape(1, -1))


def _ssd_kernel(x_ref, b_ref, c_ref, z_ref, dt_ref, hp_ref, d_ref, nw_ref, o_ref,
                state_ref, y_ref):
    L, N, P = SSD_CHUNK, SSD_D_STATE, SSD_HEAD_DIM
    G, GC, HPG = SSD_N_GROUPS, SSD_GROUP_CH, SSD_HEADS_PER_GROUP
    ci = pl.program_id(1)

    @pl.when(ci == 0)
    def _():
        state_ref[...] = jnp.zeros_like(state_ref)

    lane = lax.broadcasted_iota(jnp.int32, (1, LANES), 1)
    rows = lax.broadcasted_iota(jnp.int32, (L, L), 0)
    cols = lax.broadcasted_iota(jnp.int32, (L, L), 1)
    causal = rows >= cols
    tri = causal.astype(BF16)
    lo_half = lane < P

    dt = _softplus(dt_ref[0] + hp_ref[0:1, :])
    neg_a = jnp.where(lane < SSD_N_HEADS, -jnp.exp(hp_ref[1:2, :]), 0.0)
    a_cs = _dot_sel(tri, dt * neg_a) * LOG2E
    a_cs_t = a_cs.T
    dt_t = dt.T
    a_last = a_cs[L - 1:L, :]
    state_w = dt * jnp.exp2(a_last - a_cs)
    chunk_decay = jnp.exp2(a_last)

    def pair_cols(v, h0):
        return jnp.where(lo_half, jnp.broadcast_to(v[:, h0:h0 + 1], (v.shape[0], LANES)),
                         jnp.broadcast_to(v[:, h0 + 1:h0 + 2], (v.shape[0], LANES)))

    cb, y_off, bgt = [], [], []
    for g in range(G):
        bgb = b_ref[0, :, g * N:(g + 1) * N]
        cgb = c_ref[0, :, g * N:(g + 1) * N]
        cb.append(_dot_nt(cgb, bgb))
        y_off.append(_dot(cgb, state_ref[g].astype(BF16)))
        bgt.append(bgb.astype(F32).T.astype(BF16))

    for g in range(G):
        xs = []
        for j in range(HPG // 2):
            h0 = g * HPG + 2 * j
            sl = slice(g * GC + j * LANES, g * GC + (j + 1) * LANES)
            xp = x_ref[0, :, sl]
            xb = xp.astype(BF16)
            zero = jnp.zeros_like(xb)
            m = []
            for h in (h0, h0 + 1):
                col = jnp.broadcast_to(a_cs[:, h:h + 1], (L, L))
                decay = jnp.exp2(jnp.where(causal, col - a_cs_t[h:h + 1, :], -jnp.inf))
                m.append((cb[g] * decay * dt_t[h:h + 1, :]).astype(BF16))
            y = _dot(jnp.concatenate(m, axis=1),
                     jnp.concatenate([jnp.where(lo_half, xb, zero), jnp.where(lo_half, zero, xb)], axis=0))
            y = y + y_off[g][:, j * LANES:(j + 1) * LANES] * jnp.exp2(pair_cols(a_cs, h0))
            y = y + xp * d_ref[:, sl]
            y_ref[:, sl] = y * z_ref[0, :, sl]
            xs.append((xp * pair_cols(state_w, h0)).astype(BF16))
        cd = jnp.concatenate([pair_cols(chunk_decay, g * HPG + 2 * j) for j in range(HPG // 2)], axis=1)
        state_ref[g] = state_ref[g] * cd + _dot(bgt[g], jnp.concatenate(xs, axis=1))

    for g in range(G):
        gsl = slice(g * GC, (g + 1) * GC)
        yg = y_ref[:, gsl]
        ms = jnp.sum(yg * yg, axis=-1, keepdims=True) * (1.0 / GC)
        o_ref[0, :, gsl] = (yg * lax.rsqrt(ms + RMS_EPS) * nw_ref[:, gsl]).astype(o_ref.dtype)


def _ssd_scan(xs, bc, zs, dtp, hp, dskip, normw, name):
    B, S, _ = xs.shape
    L, G, N, DI = SSD_CHUNK, SSD_N_GROUPS, SSD_D_STATE, SSD_D_INNER
    nc = S // L
    return pl.pallas_call(
        _ssd_kernel,
        grid=(B, nc),
        in_specs=[pl.BlockSpec((1, L, DI), lambda b, c: (b, c, 0)),
                  pl.BlockSpec((1, L, SSD_BC), lambda b, c: (b, c, 0)),
                  pl.BlockSpec((1, L, SSD_BC), lambda b, c: (b, c, 1)),
                  pl.BlockSpec((1, L, DI), lambda b, c: (b, c, 0)),
                  pl.BlockSpec((1, L, LANES), lambda b, c: (b, c, 0)),
                  pl.BlockSpec((SUBLANES, LANES), lambda b, c: (0, 0)),
                  pl.BlockSpec((1, DI), lambda b, c: (0, 0)),
                  pl.BlockSpec((1, DI), lambda b, c: (0, 0))],
        out_specs=pl.BlockSpec((1, L, DI), lambda b, c: (b, c, 0)),
        out_shape=jax.ShapeDtypeStruct((B, S, DI), BF16),
        scratch_shapes=[pltpu.VMEM((G, N, SSD_GROUP_CH), F32), pltpu.VMEM((L, DI), F32)],
        compiler_params=_cparams(("parallel", "arbitrary")),
        name=name,
    )(xs, bc, bc, zs, dtp, hp, dskip, normw)


def _pad_heads(a, groups):
    per = a.shape[-1] // groups
    a = a.reshape(a.shape[:-1] + (groups, per))
    a = jnp.pad(a, [(0, 0)] * (a.ndim - 1) + [(0, LANES - per)])
    return a.reshape(a.shape[:-2] + (groups * LANES,))


def _ssd_layer(h, hb, in_w, conv_w, conv_b, dt_bias, a_log, d_skip, norm_w, out_w, ln_g, ln_b, tag):
    B, S, D = h.shape
    T = B * S
    DI = SSD_D_INNER
    o_x, o_bc, o_dt = DI, 2 * DI, DI + SSD_CONV_DIM
    w_all, lj = in_w
    w_dt =jnp.pad(w_all[lj, :, o_dt:], ((0, 0), (0, LANES - SSD_N_HEADS)))
    zs = _proj_silu(hb, w_all, lj, 0, DI, f"{tag}_z").reshape(B, S, DI)
    xs = _proj_conv(hb, w_all, lj, o_x, DI, conv_w, conv_b, 0, S, f"{tag}_x").reshape(B, S, DI)
    bc = _proj_conv(hb, w_all, lj, o_bc, 2 * SSD_BC, conv_w, conv_b, DI, S, f"{tag}_bc",
                    out_dtype=BF16).reshape(B, S, 2 * SSD_BC)
    dtp = _matmul(hb, w_dt, f"{tag}_dt").reshape(B, S, LANES)
    hp = jnp.zeros((SUBLANES, LANES), F32)
    hp = hp.at[0, :SSD_N_HEADS].set(dt_bias)
    hp = hp.at[1, :SSD_N_HEADS].set(a_log)
    dskip = jnp.repeat(d_skip, SSD_HEAD_DIM).reshape(1, DI)
    y = _ssd_scan(xs, bc, zs, dtp, hp, dskip, norm_w.reshape(1, DI), f"{tag}_scan")
    hn, hnb = _outproj_ln(y.reshape(T, SSD_D_INNER), out_w.astype(BF16), h.reshape(T, D),
                          ln_g, ln_b, f"{tag}_out")
    return hn.reshape(B, S, D), hnb


MLA_Q_SCALE = MLA_QK ** -0.5 * math.log2(math.e)


def _rope_table(pos_ref, invf_ref):
    lane = lax.broadcasted_iota(jnp.int32, (1, LANES), 1)
    ang = pos_ref[0].astype(F32) * invf_ref[...]
    q = MLA_ROPE // 2
    return jnp.where(lane < 2 * q, jnp.cos(ang), jnp.where(lane < 3 * q, -1.0, 1.0) * jnp.sin(ang))


def _rope_apply(xr, table):
    prod = xr * table
    return prod + pltpu.roll(prod, MLA_ROPE, 1)


def _rms_bf16(x, w):
    ms = jnp.mean(x * x, axis=-1, keepdims=True)
    return (x * lax.rsqrt(ms + RMS_EPS) * w).astype(BF16)


def _mla_q_kernel(qc_ref, nw_ref, w_ref, pos_ref, invf_ref, o_ref):
    xn = _rms_bf16(qc_ref[0], nw_ref[...])
    table = _rope_table(pos_ref, invf_ref)
    hw = MLA_NOPE + 2 * MLA_ROPE
    for hd in range(MLA_N_HEADS):
        r = _dot(xn, w_ref[:, hd * hw:(hd + 1) * hw])
        roped = _rope_apply(r[:, MLA_NOPE:], table)
        o_ref[0, hd, :, :MLA_NOPE] = (r[:, :MLA_NOPE] * MLA_Q_SCALE).astype(BF16)
        o_ref[0, hd, :, MLA_NOPE:] = (roped[:, :MLA_ROPE] * MLA_Q_SCALE).astype(BF16)


def _mla_kv_kernel(kvc_ref, nw_ref, w_ref, kr_ref, pos_ref, invf_ref, k_ref, v_ref):
    xn = _rms_bf16(kvc_ref[0], nw_ref[...])
    k_rope = _rope_apply(kr_ref[0], _rope_table(pos_ref, invf_ref))[:, :MLA_ROPE].astype(BF16)
    hw = MLA_NOPE + MLA_V
    for hd in range(MLA_N_HEADS):
        r = _dot(xn, w_ref[:, hd * hw:(hd + 1) * hw])
        k_ref[0, hd, :, :MLA_NOPE] = r[:, :MLA_NOPE].astype(BF16)
        k_ref[0, hd, :, MLA_NOPE:] = k_rope
        v_ref[0, hd] = r[:, MLA_NOPE:].astype(BF16)


def _mla_attn_kernel(q_ref, k_ref, v_ref, z_ref, o_ref, *, tk):
    qi = pl.program_id(2)
    halves = (q_ref[0, 0, :tk, :], q_ref[0, 0, tk:, :])
    rows = lax.broadcasted_iota(jnp.int32, (tk, tk), 0)
    cols = lax.broadcasted_iota(jnp.int32, (tk, tk), 1)

    def block(kb, carry, masked):
        start = pl.multiple_of(kb * tk, tk)
        k = k_ref[0, 0, pl.ds(start, tk), :]
        v = v_ref[0, 0, pl.ds(start, tk), :]
        live = [e for e in range(2) if masked[e] is not None]
        s = {e: _dot_nt(halves[e], k) for e in live}
        out = list(carry)
        for e in live:
            m, l, acc = carry[e]
            se = jnp.where(rows >= cols, s[e], -jnp.inf) if masked[e] else s[e]
            m_new = jnp.maximum(m, jnp.max(se, axis=-1, keepdims=True))
            p = jnp.exp2(se - m_new)
            alpha = jnp.exp2(m - m_new)
            l = alpha * l + jnp.sum(p, axis=-1, keepdims=True)
            out[e] = (m_new, l, alpha * acc + _dot(p.astype(BF16), v))
        return tuple(out)

    init = tuple((jnp.full((tk, 1), -jnp.inf, F32), jnp.zeros((tk, 1), F32), jnp.zeros((tk, MLA_V), F32))
                 for _ in range(2))
    carry = lax.fori_loop(0, 2 * qi, lambda kb, c: block(kb, c, (False, False)), init)
    carry = block(2 * qi, carry, (True, False))
    carry = block(2 * qi + 1, carry, (None, True))
    for e in range(2):
        _, l, acc = carry[e]
        rs = slice(e * tk, (e + 1) * tk)
        o_ref[0, rs, :] = (acc / l * z_ref[0, rs, :]).astype(o_ref.dtype)


def _mla_layer(h, hb, positions, in_w, q_norm_w, q_up_w, kv_norm_w, kv_up_w, out_w, ln_g, ln_b, tag):
    B, S, D = h.shape
    T = B * S
    H = MLA_N_HEADS
    o_kv = MLA_Q_RANK
    o_kr = o_kv + MLA_KV_RANK
    o_z = o_kr + MLA_ROPE
    half = MLA_ROPE // 2
    wb = in_w.astype(BF16)
    w_kr = wb[:, o_kr:o_z]
    w_kr2 = jnp.concatenate([w_kr, w_kr[:, half:], w_kr[:, :half]], axis=1)
    q_c = _matmul(hb, wb[:, :o_kv], f"{tag}_qc", tn=768).reshape(B, S, MLA_Q_RANK)
    kv_c = _matmul(hb, wb[:, o_kv:o_kr], f"{tag}_kvc").reshape(B, S, MLA_KV_RANK)
    kr = _matmul(hb, w_kr2, f"{tag}_kr").reshape(B, S, LANES)
    z = _proj_silu(hb, wb[None, :, o_z:], 0, 0, MLA_GATE, f"{tag}_z").reshape(B, S, MLA_GATE)

    wq = q_up_w.reshape(MLA_Q_RANK, H, MLA_QK)
    wq_r = wq[:, :, MLA_NOPE:]
    wq3 = jnp.concatenate([wq, wq_r[:, :, half:], wq_r[:, :, :half]], axis=2)
    wq3 = wq3.reshape(MLA_Q_RANK, H * (MLA_QK + MLA_ROPE)).astype(BF16)
    wkv = kv_up_w.astype(BF16)

    inv_freq = ROPE_THETA ** (-jnp.arange(0, MLA_ROPE, 2, dtype=F32) / MLA_ROPE)
    invf = jnp.tile(inv_freq, LANES // half).reshape(1, LANES)
    pos3 = positions.reshape(B, S, 1)

    tm = min(512, S)
    qf = pl.pallas_call(
        _mla_q_kernel,
        grid=(B, S // tm),
        in_specs=[pl.BlockSpec((1, tm, MLA_Q_RANK), lambda b, i: (b, i, 0)),
                  pl.BlockSpec((1, MLA_Q_RANK), lambda b, i: (0, 0)),
                  pl.BlockSpec(wq3.shape, lambda b, i: (0, 0)),
                  pl.BlockSpec((1, tm, 1), lambda b, i: (b, i, 0)),
                  pl.BlockSpec((1, LANES), lambda b, i: (0, 0))],
        out_specs=pl.BlockSpec((1, H, tm, MLA_QK), lambda b, i: (b, 0, i, 0)),
        out_shape=jax.ShapeDtypeStruct((B, H, S, MLA_QK), BF16),
        compiler_params=_cparams(("parallel", "parallel")),
        name=f"{tag}_qup",
    )(q_c, q_norm_w.reshape(1, -1), wq3, pos3, invf)

    kf, vf = pl.pallas_call(
        _mla_kv_kernel,
        grid=(B, S // tm),
        in_specs=[pl.BlockSpec((1, tm, MLA_KV_RANK), lambda b, i: (b, i, 0)),
                  pl.BlockSpec((1, MLA_KV_RANK), lambda b, i: (0, 0)),
                  pl.BlockSpec(wkv.shape, lambda b, i: (0, 0)),
                  pl.BlockSpec((1, tm, LANES), lambda b, i: (b, i, 0)),
                  pl.BlockSpec((1, tm, 1), lambda b, i: (b, i, 0)),
                  pl.BlockSpec((1, LANES), lambda b, i: (0, 0))],
        out_specs=[pl.BlockSpec((1, H, tm, MLA_QK), lambda b, i: (b, 0, i, 0)),
                   pl.BlockSpec((1, H, tm, MLA_V), lambda b, i: (b, 0, i, 0))],
        out_shape=[jax.ShapeDtypeStruct((B, H, S, MLA_QK), BF16),
                   jax.ShapeDtypeStruct((B, H, S, MLA_V), BF16)],
        compiler_params=_cparams(("parallel", "parallel")),
        name=f"{tag}_kvup",
    )(kv_c, kv_norm_w.reshape(1, -1), wkv, kr, pos3, invf)

    tq = min(1024, S)
    o = pl.pallas_call(
        functools.partial(_mla_attn_kernel, tk=tq // 2),
        grid=(B, H, S // tq),
        in_specs=[pl.BlockSpec((1, 1, tq, MLA_QK), lambda b, hd, i: (b, hd, i, 0)),
                  pl.BlockSpec((1, 1, S, MLA_QK), lambda b, hd, i: (b, hd, 0, 0)),
                  pl.BlockSpec((1, 1, S, MLA_V), lambda b, hd, i: (b, hd, 0, 0)),
                  pl.BlockSpec((1, tq, MLA_V), lambda b, hd, i: (b, i, hd))],
        out_specs=pl.BlockSpec((1, tq, MLA_V), lambda b, hd, i: (b, i, hd)),
        out_shape=jax.ShapeDtypeStruct((B, S, MLA_GATE), BF16),
        compiler_params=_cparams(("parallel", "parallel", "arbitrary")),
        name=f"{tag}_attn",
    )(qf, kf, vf, z)

    hn, hnb = _outproj_ln(o.reshape(T, MLA_GATE), out_w.astype(BF16), h.reshape(T, D),
                          ln_g, ln_b, f"{tag}_out")
    return hn.reshape(B, S, D), hnb


GDN_QK_PER_STEP = 4
GDN_V_PER_STEP = 2 * GDN_QK_PER_STEP
GDN_CHUNKS_PER_STEP = 8
GDN_ROWS = GDN_CHUNKS_PER_STEP * GDN_CHUNK


def _block_mask(nrows, ncols, rblk, cblk):
    r = lax.broadcasted_iota(jnp.int32, (nrows, ncols), 0) // rblk
    c = lax.broadcasted_iota(jnp.int32, (nrows, ncols), 1) // cblk
    return jnp.where(r == c, 1.0, 0.0).astype(BF16)


def _block_diag(x, reps, mask):
    return jnp.concatenate([x] * reps, axis=0) * mask


def _place_blocks(x, layout):
    zero = jnp.zeros((x.shape[0], LANES), x.dtype)
    rows = [jnp.concatenate([zero if b is None else x[:, b * LANES:(b + 1) * LANES] for b in row], axis=1)
            for row in layout]
    return jnp.concatenate(rows, axis=0)


def _gdn_kernel(q_ref, k_ref, v_ref, z_ref, b_ref, a_ref, hp_ref, nw_ref, o_ref,
                state_ref, u_s, wq_s, kd_s, qkm_s, egl_s, qn_s, kn_s):
    L, R, C = GDN_CHUNK, GDN_ROWS, GDN_CHUNKS_PER_STEP
    NQ, NV = GDN_QK_PER_STEP, GDN_V_PER_STEP
    W = NV * L
    ri = pl.program_id(2)

    @pl.when(ri == 0)
    def _():
        state_ref[...] = jnp.zeros_like(state_ref)

    beta = _sigmoid(b_ref[0])
    g = -jnp.exp(hp_ref[0, 0:1, :]) * _softplus(a_ref[0] + hp_ref[0, 1:2, :])
    rr = lax.broadcasted_iota(jnp.int32, (R, R), 0)
    cc = lax.broadcasted_iota(jnp.int32, (R, R), 1)
    blocktri = ((rr >= cc) & (rr // L == cc // L)).astype(BF16)
    gcs = _dot_sel(blocktri, g) * LOG2E
    gcs_t = gcs.T
    beta_t = beta.T

    lane = lax.broadcasted_iota(jnp.int32, (1, LANES), 1)
    lo = lane < L
    rows_w = lax.broadcasted_iota(jnp.int32, (L, W), 0)
    s_w = lax.broadcasted_iota(jnp.int32, (L, W), 1) % L
    incl_w = rows_w >= s_w
    strict_w = rows_w > s_w
    eye_w = jnp.where(rows_w == s_w, 1.0, 0.0)
    head_sel = (lax.broadcasted_iota(jnp.int32, (NV, W), 0)
                == lax.broadcasted_iota(jnp.int32, (NV, W), 1) // L)
    m_p = _block_mask(4 * L, 4 * L, L, L)
    kk_layout = [[hv // 2 if cb == hv // 2 else None for cb in range(NQ)] for hv in range(NV)]
    uw_layout = [[0, None, 2, None], [None, 1, None, 3]]
    pair_layout = [[0, None], [None, 1]]
    nw = nw_ref[...]

    def row_vec(t_heads, c):
        x = t_heads[0:NV, c * L:(c + 1) * L]
        x8 = jnp.concatenate([x] * NV, axis=1)
        return jnp.sum(jnp.where(head_sel, x8, 0.0), axis=0, keepdims=True)

    def col_wide(cols):
        return jnp.concatenate([jnp.where(lo, cols[2 * j], cols[2 * j + 1]) for j in range(NV // 2)], axis=1)

    chunks = range(C)
    rows_of = [slice(c * L, (c + 1) * L) for c in chunks]
    kq = []
    for c in chunks:
        qn, kn = [], []
        for i in range(NQ):
            q = q_ref[0, rows_of[c], i * GDN_DK:(i + 1) * GDN_DK]
            k = k_ref[0, rows_of[c], i * GDN_DK:(i + 1) * GDN_DK]
            qn.append(q * lax.rsqrt(jnp.sum(q * q, axis=-1, keepdims=True) + RMS_EPS) * (GDN_DK ** -0.5))
            kn.append(k * lax.rsqrt(jnp.sum(k * k, axis=-1, keepdims=True) + RMS_EPS))
        qn_s[c] = jnp.concatenate(qn, axis=1)
        kn_s[c] = jnp.concatenate(kn, axis=1)
        qb4 = jnp.concatenate([x.astype(BF16) for x in qn], axis=1)
        kb4 = jnp.concatenate([x.astype(BF16) for x in kn], axis=1)
        kq.append(_dot_nt(jnp.concatenate([qb4, kb4], axis=0), _place_blocks(kb4, kk_layout)))

    p_cur, t_cur = {}, {}
    for c in chunks:
        sl = rows_of[c]
        colg = [jnp.broadcast_to(gcs[sl, h:h + 1], (L, LANES)) for h in range(NV)]
        colb = [jnp.broadcast_to(beta[sl, h:h + 1], (L, LANES)) for h in range(NV)]
        decay = jnp.exp2(jnp.where(incl_w, col_wide(colg) - row_vec(gcs_t, c), -jnp.inf))
        a_w = jnp.where(strict_w, kq[c][L:] * col_wide(colb) * decay, 0.0)
        qkm_s[c] = (kq[c][:L] * decay).astype(BF16)
        for gi in range(NV // 4):
            gs = slice(gi * 4 * L, (gi + 1) * 4 * L)
            t_cur[c, gi] = eye_w[:, gs] - a_w[:, gs]
            pb = a_w[:, gs].astype(BF16)
            p_cur[c, gi] = _dot(pb, _block_diag(pb, 4, m_p))

    for it in range(5):
        for key in sorted(p_cur):
            pb = p_cur[key].astype(BF16)
            bd = _block_diag(pb, 4, m_p)
            t = t_cur[key]
            if it < 4:
                rr2 = _dot(jnp.concatenate([pb, t.astype(BF16)], axis=0), bd)
                p_cur[key] = rr2[:L]
                t_cur[key] = t + rr2[L:]
            else:
                t_cur[key] = t + _dot(t.astype(BF16), bd)

    for c in chunks:
        sl = rows_of[c]
        t_w = jnp.concatenate([t_cur[c, gi] for gi in range(NV // 4)], axis=1)
        tb_w = (t_w * row_vec(beta_t, c)).astype(BF16)
        colg = [jnp.broadcast_to(gcs[sl, h:h + 1], (L, LANES)) for h in range(NV)]
        eg = [jnp.exp2(x) for x in colg]
        gl = [gcs[c * L + L - 1:c * L + L, h:h + 1] for h in range(NV)]
        for i in range(NQ):
            h0, h1 = 2 * i, 2 * i + 1
            psl = slice(i * 2 * GDN_DV, (i + 1) * 2 * GDN_DV)
            qni = qn_s[c, :, i * GDN_DK:(i + 1) * GDN_DK]
            kni = kn_s[c, :, i * GDN_DK:(i + 1) * GDN_DK]
            base = jnp.concatenate([v_ref[0, sl, psl].astype(BF16),
                                    (kni * eg[h0]).astype(BF16), (kni * eg[h1]).astype(BF16)], axis=1)
            uw = _dot(tb_w[:, i * 2 * L:(i + 1) * 2 * L], _place_blocks(base, uw_layout))
            u_s[c, :, psl] = uw[:, :2 * GDN_DV]
            qd = jnp.concatenate([(qni * eg[h0]).astype(BF16), (qni * eg[h1]).astype(BF16)], axis=1)
            wq_s[c, i] = jnp.concatenate([uw[:, 2 * GDN_DV:].astype(BF16), qd], axis=0)
            kd_s[c, i] = jnp.concatenate([(kni * jnp.exp2(gl[h0] - colg[h0])).astype(BF16),
                                          (kni * jnp.exp2(gl[h1] - colg[h1])).astype(BF16)], axis=0)
            egl_s[c, i] = jnp.concatenate([jnp.broadcast_to(jnp.exp2(gl[h0]), (SUBLANES, GDN_DV)),
                                           jnp.broadcast_to(jnp.exp2(gl[h1]), (SUBLANES, GDN_DV))], axis=1)

    pairs = range(NQ)
    for c in chunks:
        sl = rows_of[c]
        st = [state_ref[i] for i in pairs]
        r1 = [_dot(wq_s[c, i], _place_blocks(st[i].astype(BF16), pair_layout)) for i in pairs]
        bdv = []
        for i in pairs:
            psl = slice(i * 2 * GDN_DV, (i + 1) * 2 * GDN_DV)
            vb = (u_s[c, :, psl] - r1[i][:L]).astype(BF16)
            bdv.append(_place_blocks(vb, pair_layout))
        for i in pairs:
            state_ref[i] = st[i] * egl_s[c, i, 0:1, :] + _dot_tn(kd_s[c, i], bdv[i])
        for i in pairs:
            o = r1[i][L:] + _dot(qkm_s[c, :, i * 2 * L:(i + 1) * 2 * L], bdv[i])
            for e in range(2):
                oh = o[:, e * GDN_DV:(e + 1) * GDN_DV]
                on = oh * lax.rsqrt(jnp.mean(oh * oh, axis=-1, keepdims=True) + RMS_EPS) * nw
                hsl = slice((2 * i + e) * GDN_DV, (2 * i + e + 1) * GDN_DV)
                o_ref[0, sl, hsl] = (on * z_ref[0, sl, hsl]).astype(o_ref.dtype)


def _gdn_core(qk, v, zs, bp, ap, hp, normw, name):
    B, S, _ = qk.shape
    R = min(GDN_ROWS, S)
    assert R == GDN_ROWS
    HG = GDN_N_QK_HEADS // GDN_QK_PER_STEP
    wq = GDN_QK_PER_STEP * GDN_DK
    wv = GDN_V_PER_STEP * GDN_DV
    k_blk0 = GDN_KEY_DIM // wq
    return pl.pallas_call(
        _gdn_kernel,
        grid=(B, HG, S // R),
        in_specs=[pl.BlockSpec((1, R, wq), lambda b, g, r: (b, r, g)),
                  pl.BlockSpec((1, R, wq), lambda b, g, r: (b, r, k_blk0 + g)),
                  pl.BlockSpec((1, R, wv), lambda b, g, r: (b, r, g)),
                  pl.BlockSpec((1, R, wv), lambda b, g, r: (b, r, g)),
                  pl.BlockSpec((1, R, LANES), lambda b, g, r: (b, r, g)),
                  pl.BlockSpec((1, R, LANES), lambda b, g, r: (b, r, g)),
                  pl.BlockSpec((1, SUBLANES, LANES), lambda b, g, r: (g, 0, 0)),
                  pl.BlockSpec((1, GDN_DV), lambda b, g, r: (0, 0))],
        out_specs=pl.BlockSpec((1, R, wv), lambda b, g, r: (b, r, g)),
        out_shape=jax.ShapeDtypeStruct((B, S, GDN_VAL_DIM), BF16),
        scratch_shapes=[pltpu.VMEM((GDN_QK_PER_STEP, GDN_DK, 2 * GDN_DV), F32),
                        pltpu.VMEM((GDN_CHUNKS_PER_STEP, GDN_CHUNK, wv), F32),
                        pltpu.VMEM((GDN_CHUNKS_PER_STEP, GDN_QK_PER_STEP, 2 * GDN_CHUNK, 2 * GDN_DV), BF16),
                        pltpu.VMEM((GDN_CHUNKS_PER_STEP, GDN_QK_PER_STEP, 2 * GDN_CHUNK, GDN_DK), BF16),
                        pltpu.VMEM((GDN_CHUNKS_PER_STEP, GDN_CHUNK, GDN_V_PER_STEP * GDN_CHUNK), BF16),
                        pltpu.VMEM((GDN_CHUNKS_PER_STEP, GDN_QK_PER_STEP, SUBLANES, 2 * GDN_DV), F32),
                        pltpu.VMEM((GDN_CHUNKS_PER_STEP, GDN_CHUNK, wq), F32),
                        pltpu.VMEM((GDN_CHUNKS_PER_STEP, GDN_CHUNK, wq), F32)],
        compiler_params=_cparams(("parallel", "parallel", "arbitrary")),
        name=name,
    )(qk, qk, v, zs, bp, ap, hp, normw)


def _gdn_layer(h, hb, in_w, conv_w, a_log, dt_bias, norm_w, out_w, ln_g, ln_b, tag):
    B, S, D = h.shape
    T = B * S
    o_v, o_z, o_ba = 2 * GDN_KEY_DIM, GDN_CONV_DIM, GDN_CONV_DIM + GDN_VAL_DIM
    HG = GDN_N_QK_HEADS // GDN_QK_PER_STEP
    no_bias = jnp.zeros((GDN_CONV_DIM,), F32)
    w_all, lj = in_w
    qk =_proj_conv(hb, w_all, lj, 0, o_v, conv_w, no_bias, 0, S, f"{tag}_qk").reshape(B, S, o_v)
    v = _proj_conv(hb, w_all, lj, o_v, GDN_VAL_DIM, conv_w, no_bias, o_v, S, f"{tag}_v",
                   out_dtype=BF16).reshape(B, S, GDN_VAL_DIM)
    zs = _proj_silu(hb, w_all, lj, o_z, GDN_VAL_DIM, f"{tag}_z").reshape(B, S, GDN_VAL_DIM)
    ba = _matmul(hb, w_all[lj, :, o_ba:], f"{tag}_ba").reshape(B, S, 2 * GDN_N_V_HEADS)
    bp = _pad_heads(ba[..., :GDN_N_V_HEADS], HG)
    ap = _pad_heads(ba[..., GDN_N_V_HEADS:], HG)
    hp = jnp.zeros((HG, SUBLANES, LANES), F32)
    hp = hp.at[:, 0, :GDN_V_PER_STEP].set(a_log.reshape(HG, -1))
    hp = hp.at[:, 1, :GDN_V_PER_STEP].set(dt_bias.reshape(HG, -1))
    o = _gdn_core(qk, v, zs, bp, ap, hp, norm_w.reshape(1, GDN_DV), f"{tag}_core")
    hn, hnb = _outproj_ln(o.reshape(T, GDN_VAL_DIM), out_w.astype(BF16), h.reshape(T, D),
                          ln_g, ln_b, f"{tag}_out")
    return hn.reshape(B, S, D), hnb


def kernel(x, positions, ssd_in_w, ssd_conv_w, ssd_conv_b, ssd_dt_bias, ssd_a_log, ssd_d, ssd_norm_w,
           ssd_out_w, mla_in_w, mla_q_norm_w, mla_q_up_w, mla_kv_norm_w, mla_kv_up_w, mla_out_w,
           gdn_in_w, gdn_conv_w, gdn_a_log, gdn_dt_bias, gdn_norm_w, gdn_out_w, ln_g, ln_b):
    B, S, D = x.shape
    h = x
    hb = x.reshape(B * S, D).astype(BF16)
    ssd_wb = ssd_in_w.astype(BF16)
    gdn_wb = gdn_in_w.astype(BF16)
    for i in range(DEPTH):
        kind, j = i % 3, i // 3
        if kind == 0:
            h, hb = _ssd_layer(h, hb, (ssd_wb, j), ssd_conv_w[j], ssd_conv_b[j], ssd_dt_bias[j],
                               ssd_a_log[j], ssd_d[j], ssd_norm_w[j], ssd_out_w[j], ln_g[i], ln_b[i],
                               f"l{i}_ssd")
        elif kind == 1:
            h, hb = _mla_layer(h, hb, positions, mla_in_w[j], mla_q_norm_w[j], mla_q_up_w[j],
                               mla_kv_norm_w[j], mla_kv_up_w[j], mla_out_w[j], ln_g[i], ln_b[i],
                               f"l{i}_mla")
        else:
            h, hb = _gdn_layer(h, hb, (gdn_wb, j), gdn_conv_w[j], gdn_a_log[j], gdn_dt_bias[j],
                               gdn_norm_w[j], gdn_out_w[j], ln_g[i], ln_b[i], f"l{i}_gdn")
    return h
```

```python
import functools
import math

import jax
import jax.numpy as jnp
from jax import lax
from jax.experimental import pallas as pl
from jax.experimental.pallas import tpu as pltpu

F32 = jnp.float32
BF16 = jnp.bfloat16

D_MODEL = 2048
DEPTH = 4
DEEPNORM_ALPHA = (2.0 * DEPTH) ** 0.25
LN_EPS = 1e-5
RMS_EPS = 1e-6

SSD_D_INNER = 4096
SSD_HEAD_DIM = 64
SSD_N_HEADS = 64
SSD_N_GROUPS = 8
SSD_HEADS_PER_GROUP = SSD_N_HEADS // SSD_N_GROUPS
SSD_D_STATE = 128
SSD_CONV = 4
SSD_CHUNK = 128
SSD_GROUP_CH = SSD_D_INNER // SSD_N_GROUPS
SSD_BC = SSD_N_GROUPS * SSD_D_STATE
SSD_CONV_DIM = SSD_D_INNER + 2 * SSD_BC

MLA_N_HEADS = 16
MLA_Q_RANK = 768
MLA_KV_RANK = 512
MLA_NOPE = 128
MLA_ROPE = 64
MLA_V = 128
MLA_QK = MLA_NOPE + MLA_ROPE
MLA_GATE = MLA_N_HEADS * MLA_V
ROPE_THETA = 10000.0

GDN_N_QK_HEADS = 16
GDN_N_V_HEADS = 32
GDN_DK = 128
GDN_DV = 128
GDN_KEY_DIM = GDN_N_QK_HEADS * GDN_DK
GDN_VAL_DIM = GDN_N_V_HEADS * GDN_DV
GDN_CONV = 4
GDN_CHUNK = 64
GDN_CONV_DIM = 2 * GDN_KEY_DIM + GDN_VAL_DIM

LANES = 128
SUBLANES = 8
LOG2E = math.log2(math.e)
VMEM_LIMIT_BYTES = 56 * 1024 * 1024


def _cparams(sem):
    return pltpu.CompilerParams(dimension_semantics=sem, vmem_limit_bytes=VMEM_LIMIT_BYTES)


def _sigmoid(x):
    return 1.0 / (1.0 + jnp.exp(-x))


def _silu(x):
    return x * _sigmoid(x)


def _softplus(x):
    return jnp.maximum(x, 0.0) + jnp.log1p(jnp.exp(-jnp.abs(x)))


def _dot(a, b):
    return jnp.dot(a, b, preferred_element_type=F32)


def _dot_nt(a, b):
    return lax.dot_general(a, b, (((1,), (1,)), ((), ())), preferred_element_type=F32)


def _dot_tn(a, b):
    return lax.dot_general(a, b, (((0,), (0,)), ((), ())), preferred_element_type=F32)


def _dot_sel(sel_bf16, a):
    hi = a.astype(BF16)
    r1 = a - hi.astype(F32)
    mid = r1.astype(BF16)
    lo = (r1 - mid.astype(F32)).astype(BF16)
    return _dot(sel_bf16, hi) + _dot(sel_bf16, mid) + _dot(sel_bf16, lo)


def _mm_kernel(x_ref, w_ref, o_ref):
    o_ref[...] = _dot(x_ref[...], w_ref[...].astype(BF16)).astype(o_ref.dtype)


def _pick_tile(n, pref):
    t = min(n, pref)
    while n % t:
        t -= LANES
    return t


def _matmul(x, w, name, out_dtype=F32, tm=1024, tn=512):
    T, K = x.shape
    N = w.shape[1]
    tm = min(tm, T)
    tn = _pick_tile(N, tn)
    return pl.pallas_call(
        _mm_kernel,
        grid=(T // tm, N // tn),
        in_specs=[pl.BlockSpec((tm, K), lambda i, j: (i, 0)),
                  pl.BlockSpec((K, tn), lambda i, j: (0, j))],
        out_specs=pl.BlockSpec((tm, tn), lambda i, j: (i, j)),
        out_shape=jax.ShapeDtypeStruct((T, N), out_dtype),
        compiler_params=_cparams(("parallel", "parallel")),
        name=name,
    )(x, w)


def _outproj_ln_kernel(x_ref, w_ref, h_ref, g_ref, b_ref, o_ref, ob_ref):
    k = pl.program_id(1)

    @pl.when(k == 0)
    def _():
        o_ref[...] = jnp.zeros_like(o_ref)

    o_ref[...] += _dot(x_ref[...], w_ref[...].astype(BF16))

    @pl.when(k == pl.num_programs(1) - 1)
    def _():
        r = DEEPNORM_ALPHA * h_ref[...] + o_ref[...]
        mu = jnp.mean(r, axis=-1, keepdims=True)
        d = r - mu
        var = jnp.mean(d * d, axis=-1, keepdims=True)
        out = d * lax.rsqrt(var + LN_EPS) * g_ref[...] + b_ref[...]
        o_ref[...] = out
        ob_ref[...] = out.astype(BF16)


def _outproj_ln(x, w, h, g, b, name, tm=1024, tk=512):
    T, K = x.shape
    N = w.shape[1]
    tm = min(tm, T)
    tk = min(tk, K)
    return pl.pallas_call(
        _outproj_ln_kernel,
        grid=(T // tm, K // tk),
        in_specs=[pl.BlockSpec((tm, tk), lambda i, k: (i, k)),
                  pl.BlockSpec((tk, N), lambda i, k: (k, 0)),
                  pl.BlockSpec((tm, N), lambda i, k: (i, 0)),
                  pl.BlockSpec((1, N), lambda i, k: (0, 0)),
                  pl.BlockSpec((1, N), lambda i, k: (0, 0))],
        out_specs=[pl.BlockSpec((tm, N), lambda i, k: (i, 0)),
                   pl.BlockSpec((tm, N), lambda i, k: (i, 0))],
        out_shape=[jax.ShapeDtypeStruct((T, N), F32), jax.ShapeDtypeStruct((T, N), BF16)],
        compiler_params=_cparams(("parallel", "arbitrary")),
        name=name,
    )(x, w, h, g.reshape(1, N), b.reshape(1, N))


MXU_COLS = 256


def _column_pieces(x_ref, w_ref):
    x = x_ref[...]
    starts = list(range(0, w_ref.shape[2], MXU_COLS))
    nxt = _dot(x, w_ref[0, :, :MXU_COLS].astype(BF16))
    for p, c in enumerate(starts):
        cur = nxt
        if p + 1 < len(starts):
            nxt = _dot(x, w_ref[0, :, starts[p + 1]:starts[p + 1] + MXU_COLS].astype(BF16))
        yield c, cur


def _proj_silu_kernel(x_ref, w_ref, o_ref):
    for c, y in _column_pieces(x_ref, w_ref):
        o_ref[:, c:c + MXU_COLS] = _silu(y).astype(o_ref.dtype)


def _proj_conv_kernel(x_ref, w_ref, cw_ref, cb_ref, o_ref, halo_ref, *, taps, tiles_per_seq):
    i, j = pl.program_id(0), pl.program_id(1)
    tm = x_ref.shape[0]

    @pl.when(i % tiles_per_seq == 0)
    def _():
        halo_ref[j] = jnp.zeros(halo_ref.shape[1:], F32)

    for c, y in _column_pieces(x_ref, w_ref):
        cs = slice(c, c + MXU_COLS)
        acc = cb_ref[:, cs] + cw_ref[taps - 1:taps, cs] * y
        for d in range(1, taps):
            acc = acc + cw_ref[taps - 1 - d:taps - d, cs] * pltpu.roll(y, d, 0)
        o_ref[:, cs] = _silu(acc).astype(o_ref.dtype)
        head = jnp.concatenate([halo_ref[j, :, cs], y[:SUBLANES, :]], axis=0)
        halo_ref[j, :, cs] = y[tm - SUBLANES:, :]
        acc = jnp.broadcast_to(cb_ref[:, cs], (SUBLANES, MXU_COLS))
        for k in range(taps):
            off = SUBLANES - (taps - 1) + k
            acc = acc + cw_ref[k:k + 1, cs] * head[off:off + SUBLANES]
        o_ref[:SUBLANES, cs] = _silu(acc).astype(o_ref.dtype)


def _col_tile(c0, n, pref):
    assert c0 % LANES == 0 and n % LANES == 0
    tn = min(n, pref)
    while n % tn or c0 % tn:
        tn -= LANES
    return tn


def _proj_silu(x, w, layer, c0, n, name, out_dtype=F32, tm=1024, tn=1024):
    T, K = x.shape
    tm = min(tm, T)
    tn = _col_tile(c0, n, tn)
    j0 = c0 // tn
    return pl.pallas_call(
        _proj_silu_kernel,
        grid=(T // tm, n // tn),
        in_specs=[pl.BlockSpec((tm, K), lambda i, j: (i, 0)),
                  pl.BlockSpec((1, K, tn), lambda i, j: (layer, 0, j0 + j))],
        out_specs=pl.BlockSpec((tm, tn), lambda i, j: (i, j)),
        out_shape=jax.ShapeDtypeStruct((T, n), out_dtype),
        compiler_params=_cparams(("parallel", "parallel")),
        name=name,
    )(x, w)


def _proj_conv(x, w, layer, c0, n, conv_w, conv_b, cc0, seq_len, name, out_dtype=F32, tm=1024, tn=1024):
    T, K = x.shape
    taps = conv_w.shape[0]
    tm = min(tm, seq_len)
    assert seq_len % tm == 0
    tn = _col_tile(c0, n, tn)
    assert cc0 % tn == 0
    j0, jc0 = c0 // tn, cc0 // tn
    return pl.pallas_call(
        functools.partial(_proj_conv_kernel, taps=taps, tiles_per_seq=seq_len // tm),
        grid=(T // tm, n // tn),
        in_specs=[pl.BlockSpec((tm, K), lambda i, j: (i, 0)),
                  pl.BlockSpec((1, K, tn), lambda i, j: (layer, 0, j0 + j)),
                  pl.BlockSpec((taps, tn), lambda i, j: (0, jc0 + j)),
                  pl.BlockSpec((1, tn), lambda i, j: (0, jc0 + j))],
        out_specs=pl.BlockSpec((tm, tn), lambda i, j: (i, j)),
        out_shape=jax.ShapeDtypeStruct((T, n), out_dtype),
        scratch_shapes=[pltpu.VMEM((n // tn, SUBLANES, tn), F32)],
        compiler_params=_cparams(("arbitrary", "arbitrary")),
        name=name,
    )(x, w, conv_w, conv_b.reshape(1, -1))


def _ssd_kernel(x_ref, b_ref, c_ref, z_ref, dt_ref, hp_ref, d_ref, nw_ref, o_ref,
                state_ref, y_ref):
    L, N, P = SSD_CHUNK, SSD_D_STATE, SSD_HEAD_DIM
    G, GC, HPG = SSD_N_GROUPS, SSD_GROUP_CH, SSD_HEADS_PER_GROUP
    ci = pl.program_id(1)

    @pl.when(ci == 0)
    def _():
        state_ref[...] = jnp.zeros_like(state_ref)

    lane = lax.broadcasted_iota(jnp.int32, (1, LANES), 1)
    rows = lax.broadcasted_iota(jnp.int32, (L, L), 0)
    cols = lax.broadcasted_iota(jnp.int32, (L, L), 1)
    causal = rows >= cols
    tri = causal.astype(BF16)
    lo_half = lane < P

    dt = _softplus(dt_ref[0] + hp_ref[0:1, :])
    neg_a = jnp.where(lane < SSD_N_HEADS, -jnp.exp(hp_ref[1:2, :]), 0.0)
    a_cs = _dot_sel(tri, dt * neg_a) * LOG2E
    a_cs_t = a_cs.T
    dt_t = dt.T
    a_last = a_cs[L - 1:L, :]
    state_w = dt * jnp.exp2(a_last - a_cs)
    chunk_decay = jnp.exp2(a_last)

    def pair_cols(v, h0):
        return jnp.where(lo_half, jnp.broadcast_to(v[:, h0:h0 + 1], (v.shape[0], LANES)),
                         jnp.broadcast_to(v[:, h0 + 1:h0 + 2], (v.shape[0], LANES)))

    cb, y_off, bgt = [], [], []
    for g in range(G):
        bgb = b_ref[0, :, g * N:(g + 1) * N]
        cgb = c_ref[0, :, g * N:(g + 1) * N]
        cb.append(_dot_nt(cgb, bgb))
        y_off.append(_dot(cgb, state_ref[g].astype(BF16)))
        bgt.append(bgb.astype(F32).T.astype(BF16))

    for g in range(G):
        xs = []
        for j in range(HPG // 2):
            h0 = g * HPG + 2 * j
            sl = slice(g * GC + j * LANES, g * GC + (j + 1) * LANES)
            xp = x_ref[0, :, sl]
            xb = xp.astype(BF16)
            zero = jnp.zeros_like(xb)
            m = []
            for h in (h0, h0 + 1):
                col = jnp.broadcast_to(a_cs[:, h:h + 1], (L, L))
                decay = jnp.exp2(jnp.where(causal, col - a_cs_t[h:h + 1, :], -jnp.inf))
                m.append((cb[g] * decay * dt_t[h:h + 1, :]).astype(BF16))
            y = _dot(jnp.concatenate(m, axis=1),
                     jnp.concatenate([jnp.where(lo_half, xb, zero), jnp.where(lo_half, zero, xb)], axis=0))
            y = y + y_off[g][:, j * LANES:(j + 1) * LANES] * jnp.exp2(pair_cols(a_cs, h0))
            y = y + xp * d_ref[:, sl]
            y_ref[:, sl] = y * z_ref[0, :, sl]
            xs.append((xp * pair_cols(state_w, h0)).astype(BF16))
        cd = jnp.concatenate([pair_cols(chunk_decay, g * HPG + 2 * j) for j in range(HPG // 2)], axis=1)
        state_ref[g] = state_ref[g] * cd + _dot(bgt[g], jnp.concatenate(xs, axis=1))

    for g in range(G):
        gsl = slice(g * GC, (g + 1) * GC)
        yg = y_ref[:, gsl]
        ms = jnp.sum(yg * yg, axis=-1, keepdims=True) * (1.0 / GC)
        o_ref[0, :, gsl] = (yg * lax.rsqrt(ms + RMS_EPS) * nw_ref[:, gsl]).astype(o_ref.dtype)


def _ssd_scan(xs, bc, zs, dtp, hp, dskip, normw, name):
    B, S, _ = xs.shape
    L, G, N, DI = SSD_CHUNK, SSD_N_GROUPS, SSD_D_STATE, SSD_D_INNER
    nc = S // L
    return pl.pallas_call(
        _ssd_kernel,
        grid=(B, nc),
        in_specs=[pl.BlockSpec((1, L, DI), lambda b, c: (b, c, 0)),
                  pl.BlockSpec((1, L, SSD_BC), lambda b, c: (b, c, 0)),
                  pl.BlockSpec((1, L, SSD_BC), lambda b, c: (b, c, 1)),
                  pl.BlockSpec((1, L, DI), lambda b, c: (b, c, 0)),
                  pl.BlockSpec((1, L, LANES), lambda b, c: (b, c, 0)),
                  pl.BlockSpec((SUBLANES, LANES), lambda b, c: (0, 0)),
                  pl.BlockSpec((1, DI), lambda b, c: (0, 0)),
                  pl.BlockSpec((1, DI), lambda b, c: (0, 0))],
        out_specs=pl.BlockSpec((1, L, DI), lambda b, c: (b, c, 0)),
        out_shape=jax.ShapeDtypeStruct((B, S, DI), BF16),
        scratch_shapes=[pltpu.VMEM((G, N, SSD_GROUP_CH), F32), pltpu.VMEM((L, DI), F32)],
        compiler_params=_cparams(("parallel", "arbitrary")),
        name=name,
    )(xs, bc, bc, zs, dtp, hp, dskip, normw)


def _pad_heads(a, groups):
    per = a.shape[-1] // groups
    a = a.reshape(a.shape[:-1] + (groups, per))
    a = jnp.pad(a, [(0, 0)] * (a.ndim - 1) + [(0, LANES - per)])
    return a.reshape(a.shape[:-2] + (groups * LANES,))


def _ssd_layer(h, hb, in_w, conv_w, conv_b, dt_bias, a_log, d_skip, norm_w, out_w, ln_g, ln_b, tag):
    B, S, D = h.shape
    T = B * S
    DI = SSD_D_INNER
    o_x, o_bc, o_dt = DI, 2 * DI, DI + SSD_CONV_DIM
    w_all, lj = in_w
    w_dt =jnp.pad(w_all[lj, :, o_dt:], ((0, 0), (0, LANES - SSD_N_HEADS)))
    zs = _proj_silu(hb, w_all, lj, 0, DI, f"{tag}_z").reshape(B, S, DI)
    xs = _proj_conv(hb, w_all, lj, o_x, DI, conv_w, conv_b, 0, S, f"{tag}_x").reshape(B, S, DI)
    bc = _proj_conv(hb, w_all, lj, o_bc, 2 * SSD_BC, conv_w, conv_b, DI, S, f"{tag}_bc",
                    out_dtype=BF16).reshape(B, S, 2 * SSD_BC)
    dtp = _matmul(hb, w_dt, f"{tag}_dt").reshape(B, S, LANES)
    hp = jnp.zeros((SUBLANES, LANES), F32)
    hp = hp.at[0, :SSD_N_HEADS].set(dt_bias)
    hp = hp.at[1, :SSD_N_HEADS].set(a_log)
    dskip = jnp.repeat(d_skip, SSD_HEAD_DIM).reshape(1, DI)
    y = _ssd_scan(xs, bc, zs, dtp, hp, dskip, norm_w.reshape(1, DI), f"{tag}_scan")
    hn, hnb = _outproj_ln(y.reshape(T, SSD_D_INNER), out_w.astype(BF16), h.reshape(T, D),
                          ln_g, ln_b, f"{tag}_out")
    return hn.reshape(B, S, D), hnb


MLA_Q_SCALE = MLA_QK ** -0.5 * math.log2(math.e)


def _rope_table(pos_ref, invf_ref):
    lane = lax.broadcasted_iota(jnp.int32, (1, LANES), 1)
    ang = pos_ref[0].astype(F32) * invf_ref[...]
    q = MLA_ROPE // 2
    return jnp.where(lane < 2 * q, jnp.cos(ang), jnp.where(lane < 3 * q, -1.0, 1.0) * jnp.sin(ang))


def _rope_apply(xr, table):
    prod = xr * table
    return prod + pltpu.roll(prod, MLA_ROPE, 1)


def _rms_bf16(x, w):
    ms = jnp.mean(x * x, axis=-1, keepdims=True)
    return (x * lax.rsqrt(ms + RMS_EPS) * w).astype(BF16)


def _mla_q_kernel(qc_ref, nw_ref, w_ref, pos_ref, invf_ref, o_ref):
    xn = _rms_bf16(qc_ref[0], nw_ref[...])
    table = _rope_table(pos_ref, invf_ref)
    hw = MLA_NOPE + 2 * MLA_ROPE
    for hd in range(MLA_N_HEADS):
        r = _dot(xn, w_ref[:, hd * hw:(hd + 1) * hw])
        roped = _rope_apply(r[:, MLA_NOPE:], table)
        o_ref[0, hd, :, :MLA_NOPE] = (r[:, :MLA_NOPE] * MLA_Q_SCALE).astype(BF16)
        o_ref[0, hd, :, MLA_NOPE:] = (roped[:, :MLA_ROPE] * MLA_Q_SCALE).astype(BF16)


def _mla_kv_kernel(kvc_ref, nw_ref, w_ref, kr_ref, pos_ref, invf_ref, k_ref, v_ref):
    xn = _rms_bf16(kvc_ref[0], nw_ref[...])
    k_rope = _rope_apply(kr_ref[0], _rope_table(pos_ref, invf_ref))[:, :MLA_ROPE].astype(BF16)
    hw = MLA_NOPE + MLA_V
    for hd in range(MLA_N_HEADS):
        r = _dot(xn, w_ref[:, hd * hw:(hd + 1) * hw])
        k_ref[0, hd, :, :MLA_NOPE] = r[:, :MLA_NOPE].astype(BF16)
        k_ref[0, hd, :, MLA_NOPE:] = k_rope
        v_ref[0, hd] = r[:, MLA_NOPE:].astype(BF16)


def _mla_attn_kernel(q_ref, k_ref, v_ref, z_ref, o_ref, *, tk):
    qi = pl.program_id(2)
    ns = q_ref.shape[2] // tk
    halves = tuple(q_ref[0, 0, e * tk:(e + 1) * tk, :] for e in range(ns))
    rows = lax.broadcasted_iota(jnp.int32, (tk, tk), 0)
    cols = lax.broadcasted_iota(jnp.int32, (tk, tk), 1)

    def block(kb, carry, masked):
        start = pl.multiple_of(kb * tk, tk)
        k = k_ref[0, 0, pl.ds(start, tk), :]
        v = v_ref[0, 0, pl.ds(start, tk), :]
        live = [e for e in range(ns) if masked[e] is not None]
        s = {e: _dot_nt(halves[e], k) for e in live}
        out = list(carry)
        for e in live:
            m, l, acc = carry[e]
            se = jnp.where(rows >= cols, s[e], -jnp.inf) if masked[e] else s[e]
            m_new = jnp.maximum(m, jnp.max(se, axis=-1, keepdims=True))
            p = jnp.exp2(se - m_new)
            alpha = jnp.exp2(m - m_new)
            l = alpha * l + jnp.sum(p, axis=-1, keepdims=True)
            out[e] = (m_new, l, alpha * acc + _dot(p.astype(BF16), v))
        return tuple(out)

    init = tuple((jnp.full((tk, 1), -jnp.inf, F32), jnp.zeros((tk, 1), F32), jnp.zeros((tk, MLA_V), F32))
                 for _ in range(ns))
    carry = lax.fori_loop(0, ns * qi, lambda kb, c: block(kb, c, (False,) * ns), init)
    for t in range(ns):
        carry = block(ns * qi + t, carry, tuple(None if e < t else (e == t) for e in range(ns)))
    for e in range(ns):
        _, l, acc = carry[e]
        rs = slice(e * tk, (e + 1) * tk)
        o_ref[0, rs, :] = (acc / l * z_ref[0, rs, :]).astype(o_ref.dtype)


def _mla_layer(h, hb, positions, in_w, q_norm_w, q_up_w, kv_norm_w, kv_up_w, out_w, ln_g, ln_b, tag):
    B, S, D = h.shape
    T = B * S
    H = MLA_N_HEADS
    o_kv = MLA_Q_RANK
    o_kr = o_kv + MLA_KV_RANK
    o_z = o_kr + MLA_ROPE
    half = MLA_ROPE // 2
    wb = in_w.astype(BF16)
    w_kr = wb[:, o_kr:o_z]
    w_kr2 = jnp.concatenate([w_kr, w_kr[:, half:], w_kr[:, :half]], axis=1)
    q_c = _matmul(hb, wb[:, :o_kv], f"{tag}_qc", tn=768).reshape(B, S, MLA_Q_RANK)
    kv_c = _matmul(hb, wb[:, o_kv:o_kr], f"{tag}_kvc").reshape(B, S, MLA_KV_RANK)
    kr = _matmul(hb, w_kr2, f"{tag}_kr").reshape(B, S, LANES)
    z = _proj_silu(hb, wb[None, :, o_z:], 0, 0, MLA_GATE, f"{tag}_z").reshape(B, S, MLA_GATE)

    wq = q_up_w.reshape(MLA_Q_RANK, H, MLA_QK)
    wq_r = wq[:, :, MLA_NOPE:]
    wq3 = jnp.concatenate([wq, wq_r[:, :, half:], wq_r[:, :, :half]], axis=2)
    wq3 = wq3.reshape(MLA_Q_RANK, H * (MLA_QK + MLA_ROPE)).astype(BF16)
    wkv = kv_up_w.astype(BF16)

    inv_freq = ROPE_THETA ** (-jnp.arange(0, MLA_ROPE, 2, dtype=F32) / MLA_ROPE)
    invf = jnp.tile(inv_freq, LANES // half).reshape(1, LANES)
    pos3 = positions.reshape(B, S, 1)

    tm = min(512, S)
    qf = pl.pallas_call(
        _mla_q_kernel,
        grid=(B, S // tm),
        in_specs=[pl.BlockSpec((1, tm, MLA_Q_RANK), lambda b, i: (b, i, 0)),
                  pl.BlockSpec((1, MLA_Q_RANK), lambda b, i: (0, 0)),
                  pl.BlockSpec(wq3.shape, lambda b, i: (0, 0)),
                  pl.BlockSpec((1, tm, 1), lambda b, i: (b, i, 0)),
                  pl.BlockSpec((1, LANES), lambda b, i: (0, 0))],
        out_specs=pl.BlockSpec((1, H, tm, MLA_QK), lambda b, i: (b, 0, i, 0)),
        out_shape=jax.ShapeDtypeStruct((B, H, S, MLA_QK), BF16),
        compiler_params=_cparams(("parallel", "parallel")),
        name=f"{tag}_qup",
    )(q_c, q_norm_w.reshape(1, -1), wq3, pos3, invf)

    kf, vf = pl.pallas_call(
        _mla_kv_kernel,
        grid=(B, S // tm),
        in_specs=[pl.BlockSpec((1, tm, MLA_KV_RANK), lambda b, i: (b, i, 0)),
                  pl.BlockSpec((1, MLA_KV_RANK), lambda b, i: (0, 0)),
                  pl.BlockSpec(wkv.shape, lambda b, i: (0, 0)),
                  pl.BlockSpec((1, tm, LANES), lambda b, i: (b, i, 0)),
                  pl.BlockSpec((1, tm, 1), lambda b, i: (b, i, 0)),
                  pl.BlockSpec((1, LANES), lambda b, i: (0, 0))],
        out_specs=[pl.BlockSpec((1, H, tm, MLA_QK), lambda b, i: (b, 0, i, 0)),
                   pl.BlockSpec((1, H, tm, MLA_V), lambda b, i: (b, 0, i, 0))],
        out_shape=[jax.ShapeDtypeStruct((B, H, S, MLA_QK), BF16),
                   jax.ShapeDtypeStruct((B, H, S, MLA_V), BF16)],
        compiler_params=_cparams(("parallel", "parallel")),
        name=f"{tag}_kvup",
    )(kv_c, kv_norm_w.reshape(1, -1), wkv, kr, pos3, invf)

    tq = min(2048, S)
    o = pl.pallas_call(
        functools.partial(_mla_attn_kernel, tk=tq // 4),
        grid=(B, H, S // tq),
        in_specs=[pl.BlockSpec((1, 1, tq, MLA_QK), lambda b, hd, i: (b, hd, i, 0)),
                  pl.BlockSpec((1, 1, S, MLA_QK), lambda b, hd, i: (b, hd, 0, 0)),
                  pl.BlockSpec((1, 1, S, MLA_V), lambda b, hd, i: (b, hd, 0, 0)),
                  pl.BlockSpec((1, tq, MLA_V), lambda b, hd, i: (b, i, hd))],
        out_specs=pl.BlockSpec((1, tq, MLA_V), lambda b, hd, i: (b, i, hd)),
        out_shape=jax.ShapeDtypeStruct((B, S, MLA_GATE), BF16),
        compiler_params=_cparams(("parallel", "parallel", "arbitrary")),
        name=f"{tag}_attn",
    )(qf, kf, vf, z)

    hn, hnb = _outproj_ln(o.reshape(T, MLA_GATE), out_w.astype(BF16), h.reshape(T, D),
                          ln_g, ln_b, f"{tag}_out")
    return hn.reshape(B, S, D), hnb


GDN_QK_PER_STEP = 4
GDN_V_PER_STEP = 2 * GDN_QK_PER_STEP
GDN_CHUNKS_PER_STEP = 8
GDN_ROWS = GDN_CHUNKS_PER_STEP * GDN_CHUNK


def _block_mask(nrows, ncols, rblk, cblk):
    r = lax.broadcasted_iota(jnp.int32, (nrows, ncols), 0) // rblk
    c = lax.broadcasted_iota(jnp.int32, (nrows, ncols), 1) // cblk
    return jnp.where(r == c, 1.0, 0.0).astype(BF16)


def _block_diag(x, reps, mask):
    return jnp.concatenate([x] * reps, axis=0) * mask


def _place_blocks(x, layout):
    zero = jnp.zeros((x.shape[0], LANES), x.dtype)
    rows = [jnp.concatenate([zero if b is None else x[:, b * LANES:(b + 1) * LANES] for b in row], axis=1)
            for row in layout]
    return jnp.concatenate(rows, axis=0)


def _gdn_kernel(q_ref, k_ref, v_ref, z_ref, b_ref, a_ref, hp_ref, nw_ref, o_ref,
                state_ref, u_s, wq_s, kd_s, qkm_s, egl_s, qn_s, kn_s):
    L, R, C = GDN_CHUNK, GDN_ROWS, GDN_CHUNKS_PER_STEP
    NQ, NV = GDN_QK_PER_STEP, GDN_V_PER_STEP
    W = NV * L
    ri = pl.program_id(2)

    @pl.when(ri == 0)
    def _():
        state_ref[...] = jnp.zeros_like(state_ref)

    beta = _sigmoid(b_ref[0])
    g = -jnp.exp(hp_ref[0, 0:1, :]) * _softplus(a_ref[0] + hp_ref[0, 1:2, :])
    rr = lax.broadcasted_iota(jnp.int32, (R, R), 0)
    cc = lax.broadcasted_iota(jnp.int32, (R, R), 1)
    blocktri = ((rr >= cc) & (rr // L == cc // L)).astype(BF16)
    gcs = _dot_sel(blocktri, g) * LOG2E
    gcs_t = gcs.T
    beta_t = beta.T

    lane = lax.broadcasted_iota(jnp.int32, (1, LANES), 1)
    lo = lane < L
    rows_w = lax.broadcasted_iota(jnp.int32, (L, W), 0)
    s_w = lax.broadcasted_iota(jnp.int32, (L, W), 1) % L
    incl_w = rows_w >= s_w
    strict_w = rows_w > s_w
    eye_w = jnp.where(rows_w == s_w, 1.0, 0.0)
    head_sel = (lax.broadcasted_iota(jnp.int32, (NV, W), 0)
                == lax.broadcasted_iota(jnp.int32, (NV, W), 1) // L)
    m_p = _block_mask(4 * L, 4 * L, L, L)
    kk_layout = [[hv // 2 if cb == hv // 2 else None for cb in range(NQ)] for hv in range(NV)]
    uw_layout = [[0, None, 2, None], [None, 1, None, 3]]
    pair_layout = [[0, None], [None, 1]]
    nw = nw_ref[...]

    def row_vec(t_heads, c):
        x = t_heads[0:NV, c * L:(c + 1) * L]
        x8 = jnp.concatenate([x] * NV, axis=1)
        return jnp.sum(jnp.where(head_sel, x8, 0.0), axis=0, keepdims=True)

    def col_wide(cols):
        return jnp.concatenate([jnp.where(lo, cols[2 * j], cols[2 * j + 1]) for j in range(NV // 2)], axis=1)

    chunks = range(C)
    rows_of = [slice(c * L, (c + 1) * L) for c in chunks]
    kq = []
    for c in chunks:
        qn, kn = [], []
        for i in range(NQ):
            q = q_ref[0, rows_of[c], i * GDN_DK:(i + 1) * GDN_DK]
            k = k_ref[0, rows_of[c], i * GDN_DK:(i + 1) * GDN_DK]
            qn.append(q * lax.rsqrt(jnp.sum(q * q, axis=-1, keepdims=True) + RMS_EPS) * (GDN_DK ** -0.5))
            kn.append(k * lax.rsqrt(jnp.sum(k * k, axis=-1, keepdims=True) + RMS_EPS))
        qn_s[c] = jnp.concatenate(qn, axis=1)
        kn_s[c] = jnp.concatenate(kn, axis=1)
        qb4 = jnp.concatenate([x.astype(BF16) for x in qn], axis=1)
        kb4 = jnp.concatenate([x.astype(BF16) for x in kn], axis=1)
        kq.append(_dot_nt(jnp.concatenate([qb4, kb4], axis=0), _place_blocks(kb4, kk_layout)))

    p_cur, t_cur = {}, {}
    for c in chunks:
        sl = rows_of[c]
        colg = [jnp.broadcast_to(gcs[sl, h:h + 1], (L, LANES)) for h in range(NV)]
        colb = [jnp.broadcast_to(beta[sl, h:h + 1], (L, LANES)) for h in range(NV)]
        decay = jnp.exp2(jnp.where(incl_w, col_wide(colg) - row_vec(gcs_t, c), -jnp.inf))
        a_w = jnp.where(strict_w, kq[c][L:] * col_wide(colb) * decay, 0.0)
        qkm_s[c] = (kq[c][:L] * decay).astype(BF16)
        for gi in range(NV // 4):
            gs = slice(gi * 4 * L, (gi + 1) * 4 * L)
            t_cur[c, gi] = eye_w[:, gs] - a_w[:, gs]
            pb = a_w[:, gs].astype(BF16)
            p_cur[c, gi] = _dot(pb, _block_diag(pb, 4, m_p))

    for it in range(5):
        for key in sorted(p_cur):
            pb = p_cur[key].astype(BF16)
            bd = _block_diag(pb, 4, m_p)
            t = t_cur[key]
            if it < 4:
                rr2 = _dot(jnp.concatenate([pb, t.astype(BF16)], axis=0), bd)
                p_cur[key] = rr2[:L]
                t_cur[key] = t + rr2[L:]
            else:
                t_cur[key] = t + _dot(t.astype(BF16), bd)

    for c in chunks:
        sl = rows_of[c]
        t_w = jnp.concatenate([t_cur[c, gi] for gi in range(NV // 4)], axis=1)
        tb_w = (t_w * row_vec(beta_t, c)).astype(BF16)
        colg = [jnp.broadcast_to(gcs[sl, h:h + 1], (L, LANES)) for h in range(NV)]
        eg = [jnp.exp2(x) for x in colg]
        gl = [gcs[c * L + L - 1:c * L + L, h:h + 1] for h in range(NV)]
        for i in range(NQ):
            h0, h1 = 2 * i, 2 * i + 1
            psl = slice(i * 2 * GDN_DV, (i + 1) * 2 * GDN_DV)
            qni = qn_s[c, :, i * GDN_DK:(i + 1) * GDN_DK]
            kni = kn_s[c, :, i * GDN_DK:(i + 1) * GDN_DK]
            base = jnp.concatenate([v_ref[0, sl, psl].astype(BF16),
                                    (kni * eg[h0]).astype(BF16), (kni * eg[h1]).astype(BF16)], axis=1)
            uw = _dot(tb_w[:, i * 2 * L:(i + 1) * 2 * L], _place_blocks(base, uw_layout))
            u_s[c, :, psl] = uw[:, :2 * GDN_DV]
            qd = jnp.concatenate([(qni * eg[h0]).astype(BF16), (qni * eg[h1]).astype(BF16)], axis=1)
            wq_s[c, i] = jnp.concatenate([uw[:, 2 * GDN_DV:].astype(BF16), qd], axis=0)
            kd_s[c, i] = jnp.concatenate([(kni * jnp.exp2(gl[h0] - colg[h0])).astype(BF16),
                                          (kni * jnp.exp2(gl[h1] - colg[h1])).astype(BF16)], axis=0)
            egl_s[c, i] = jnp.concatenate([jnp.broadcast_to(jnp.exp2(gl[h0]), (SUBLANES, GDN_DV)),
                                           jnp.broadcast_to(jnp.exp2(gl[h1]), (SUBLANES, GDN_DV))], axis=1)

    pairs = range(NQ)
    for c in chunks:
        sl = rows_of[c]
        st = [state_ref[i] for i in pairs]
        r1 = [_dot(wq_s[c, i], _place_blocks(st[i].astype(BF16), pair_layout)) for i in pairs]
        bdv = []
        for i in pairs:
            psl = slice(i * 2 * GDN_DV, (i + 1) * 2 * GDN_DV)
            vb = (u_s[c, :, psl] - r1[i][:L]).astype(BF16)
            bdv.append(_place_blocks(vb, pair_layout))
        for i in pairs:
            state_ref[i] = st[i] * egl_s[c, i, 0:1, :] + _dot_tn(kd_s[c, i], bdv[i])
        for i in pairs:
            o = r1[i][L:] + _dot(qkm_s[c, :, i * 2 * L:(i + 1) * 2 * L], bdv[i])
            for e in range(2):
                oh = o[:, e * GDN_DV:(e + 1) * GDN_DV]
                on = oh * lax.rsqrt(jnp.mean(oh * oh, axis=-1, keepdims=True) + RMS_EPS) * nw
                hsl = slice((2 * i + e) * GDN_DV, (2 * i + e + 1) * GDN_DV)
                o_ref[0, sl, hsl] = (on * z_ref[0, sl, hsl]).astype(o_ref.dtype)


def _gdn_core(qk, v, zs, bp, ap, hp, normw, name):
    B, S, _ = qk.shape
    R = min(GDN_ROWS, S)
    assert R == GDN_ROWS
    HG = GDN_N_QK_HEADS // GDN_QK_PER_STEP
    wq = GDN_QK_PER_STEP * GDN_DK
    wv = GDN_V_PER_STEP * GDN_DV
    k_blk0 = GDN_KEY_DIM // wq
    return pl.pallas_call(
        _gdn_kernel,
        grid=(B, HG, S // R),
        in_specs=[pl.BlockSpec((1, R, wq), lambda b, g, r: (b, r, g)),
                  pl.BlockSpec((1, R, wq), lambda b, g, r: (b, r, k_blk0 + g)),
                  pl.BlockSpec((1, R, wv), lambda b, g, r: (b, r, g)),
                  pl.BlockSpec((1, R, wv), lambda b, g, r: (b, r, g)),
                  pl.BlockSpec((1, R, LANES), lambda b, g, r: (b, r, g)),
                  pl.BlockSpec((1, R, LANES), lambda b, g, r: (b, r, g)),
                  pl.BlockSpec((1, SUBLANES, LANES), lambda b, g, r: (g, 0, 0)),
                  pl.BlockSpec((1, GDN_DV), lambda b, g, r: (0, 0))],
        out_specs=pl.BlockSpec((1, R, wv), lambda b, g, r: (b, r, g)),
        out_shape=jax.ShapeDtypeStruct((B, S, GDN_VAL_DIM), BF16),
        scratch_shapes=[pltpu.VMEM((GDN_QK_PER_STEP, GDN_DK, 2 * GDN_DV), F32),
                        pltpu.VMEM((GDN_CHUNKS_PER_STEP, GDN_CHUNK, wv), F32),
                        pltpu.VMEM((GDN_CHUNKS_PER_STEP, GDN_QK_PER_STEP, 2 * GDN_CHUNK, 2 * GDN_DV), BF16),
                        pltpu.VMEM((GDN_CHUNKS_PER_STEP, GDN_QK_PER_STEP, 2 * GDN_CHUNK, GDN_DK), BF16),
                        pltpu.VMEM((GDN_CHUNKS_PER_STEP, GDN_CHUNK, GDN_V_PER_STEP * GDN_CHUNK), BF16),
                        pltpu.VMEM((GDN_CHUNKS_PER_STEP, GDN_QK_PER_STEP, SUBLANES, 2 * GDN_DV), F32),
                        pltpu.VMEM((GDN_CHUNKS_PER_STEP, GDN_CHUNK, wq), F32),
                        pltpu.VMEM((GDN_CHUNKS_PER_STEP, GDN_CHUNK, wq), F32)],
        compiler_params=_cparams(("parallel", "parallel", "arbitrary")),
        name=name,
    )(qk, qk, v, zs, bp, ap, hp, normw)


def _gdn_layer(h, hb, in_w, conv_w, a_log, dt_bias, norm_w, out_w, ln_g, ln_b, tag):
    B, S, D = h.shape
    T = B * S
    o_v, o_z, o_ba = 2 * GDN_KEY_DIM, GDN_CONV_DIM, GDN_CONV_DIM + GDN_VAL_DIM
    HG = GDN_N_QK_HEADS // GDN_QK_PER_STEP
    no_bias = jnp.zeros((GDN_CONV_DIM,), F32)
    w_all, lj = in_w
    qk =_proj_conv(hb, w_all, lj, 0, o_v, conv_w, no_bias, 0, S, f"{tag}_qk").reshape(B, S, o_v)
    v = _proj_conv(hb, w_all, lj, o_v, GDN_VAL_DIM, conv_w, no_bias, o_v, S, f"{tag}_v",
                   out_dtype=BF16).reshape(B, S, GDN_VAL_DIM)
    zs = _proj_silu(hb, w_all, lj, o_z, GDN_VAL_DIM, f"{tag}_z").reshape(B, S, GDN_VAL_DIM)
    ba = _matmul(hb, w_all[lj, :, o_ba:], f"{tag}_ba").reshape(B, S, 2 * GDN_N_V_HEADS)
    bp = _pad_heads(ba[..., :GDN_N_V_HEADS], HG)
    ap = _pad_heads(ba[..., GDN_N_V_HEADS:], HG)
    hp = jnp.zeros((HG, SUBLANES, LANES), F32)
    hp = hp.at[:, 0, :GDN_V_PER_STEP].set(a_log.reshape(HG, -1))
    hp = hp.at[:, 1, :GDN_V_PER_STEP].set(dt_bias.reshape(HG, -1))
    o = _gdn_core(qk, v, zs, bp, ap, hp, norm_w.reshape(1, GDN_DV), f"{tag}_core")
    hn, hnb = _outproj_ln(o.reshape(T, GDN_VAL_DIM), out_w.astype(BF16), h.reshape(T, D),
                          ln_g, ln_b, f"{tag}_out")
    return hn.reshape(B, S, D), hnb


def kernel(x, positions, ssd_in_w, ssd_conv_w, ssd_conv_b, ssd_dt_bias, ssd_a_log, ssd_d, ssd_norm_w,
           ssd_out_w, mla_in_w, mla_q_norm_w, mla_q_up_w, mla_kv_norm_w, mla_kv_up_w, mla_out_w,
           gdn_in_w, gdn_conv_w, gdn_a_log, gdn_dt_bias, gdn_norm_w, gdn_out_w, ln_g, ln_b):
    B, S, D = x.shape
    h = x
    hb = x.reshape(B * S, D).astype(BF16)
    ssd_wb = ssd_in_w.astype(BF16)
    gdn_wb = gdn_in_w.astype(BF16)
    for i in range(DEPTH):
        kind, j = i % 3, i // 3
        if kind == 0:
            h, hb = _ssd_layer(h, hb, (ssd_wb, j), ssd_conv_w[j], ssd_conv_b[j], ssd_dt_bias[j],
                               ssd_a_log[j], ssd_d[j], ssd_norm_w[j], ssd_out_w[j], ln_g[i], ln_b[i],
                               f"l{i}_ssd")
        elif kind == 1:
            h, hb = _mla_layer(h, hb, positions, mla_in_w[j], mla_q_norm_w[j], mla_q_up_w[j],
                               mla_kv_norm_w[j], mla_kv_up_w[j], mla_out_w[j], ln_g[i], ln_b[i],
                               f"l{i}_mla")
        else:
            h, hb = _gdn_layer(h, hb, (gdn_wb, j), gdn_conv_w[j], gdn_a_log[j], gdn_dt_bias[j],
                               gdn_norm_w[j], gdn_out_w[j], ln_g[i], ln_b[i], f"l{i}_gdn")
    return h
```
